```python
import math
import jax, jax.numpy as jnp
from jax import lax
import numpy as np

D_MODEL = 2048
BATCH = 8
SEQ = 2048
DEPTH = 1

HEAD_DIM = 64
N_HEADS_SWA = 16
N_KV_SWA = 4
N_HEADS_SB = 16
WINDOW = 128
BLOCK = 128
MEM_LEN = 256
N_HEADS_MEM = 4
HEAD_DIM_MEM = D_MODEL // N_HEADS_MEM
D_SWA = N_HEADS_SWA * HEAD_DIM
D_KV_SWA = N_KV_SWA * HEAD_DIM
D_SB = N_HEADS_SB * HEAD_DIM
D_MIX = D_SWA + D_SB
D_IN = D_SWA + 2 * D_KV_SWA + 3 * D_SB
D_FF = -(-8 * D_MODEL // (3 * 256)) * 256
ALPHA = (2.0 * DEPTH) ** 0.25
BETA = (8.0 * DEPTH) ** -0.25
LN_EPS = 1e-5
RMS_EPS = 1e-6

kernel_name = "hymba_swa_sink_stickbreak_deepnorm_layer"


def _alibi_slopes(n):
    return jnp.asarray(2.0 ** (-8.0 * np.arange(1, n + 1) / n), dtype=jnp.float32)


def layer_norm(x, g, b):
    xf = x.astype(jnp.float32)
    mu = jnp.mean(xf, axis=-1, keepdims=True)
    var = jnp.mean(jnp.square(xf - mu), axis=-1, keepdims=True)
    y = (xf - mu) * lax.rsqrt(var + LN_EPS)
    return (y * g.astype(jnp.float32) + b.astype(jnp.float32)).astype(x.dtype)


def head_rmsnorm(o, g):
    H, D = o.shape[-2:]
    of = o.astype(jnp.float32)
    y = of * lax.rsqrt(jnp.mean(jnp.square(of), axis=-1, keepdims=True) + RMS_EPS)
    return (y * g.reshape(H, D).astype(jnp.float32)).astype(o.dtype)


def swa_sink_attention(q, k, v, sinks):
    B, S, HQ, D = q.shape
    HKV = k.shape[2]
    G = HQ // HKV
    nb = S // BLOCK
    qb = q.reshape(B, nb, BLOCK, HKV, G, D)

    def band(t):
        tb = t.reshape(B, nb, BLOCK, HKV, D)
        prev = jnp.pad(tb, ((0, 0), (1, 0), (0, 0), (0, 0), (0, 0)))[:, :-1]
        return jnp.concatenate([prev, tb], axis=2)

    kb, vb = band(k), band(v)
    scores = jnp.einsum('bnqkgd,bnskd->bnkgqs', qb, kb).astype(jnp.float32) / math.sqrt(D)
    qi = jnp.arange(BLOCK)[:, None]
    kj = jnp.arange(2 * BLOCK)[None, :]
    dist = qi + BLOCK - kj
    key_pos = jnp.arange(nb)[:, None, None] * BLOCK - BLOCK + kj
    valid = (dist >= 0) & (dist < WINDOW) & (key_pos >= 0)
    slopes = _alibi_slopes(HQ).reshape(HKV, G)
    scores = scores - slopes[None, None, :, :, None, None] * dist.astype(jnp.float32)
    scores = jnp.where(valid[None, :, None, None], scores, -jnp.inf)
    sink = sinks.astype(jnp.float32).reshape(HKV, G)[None, None, :, :, None, None]
    m = jnp.maximum(jnp.max(scores, axis=-1, keepdims=True), sink)
    p = jnp.exp(scores - m)
    denom = jnp.sum(p, axis=-1, keepdims=True) + jnp.exp(sink - m)
    out = jnp.einsum('bnkgqs,bnskd->bnqkgd', (p / denom).astype(v.dtype), vb)
    return out.reshape(B, S, HQ, D)


def stick_breaking_attention(q, k, v):
    B, S, H, D = q.shape
    nb = S // BLOCK
    outs = []
    for n in range(nb):
        t0, t1 = n * BLOCK, (n + 1) * BLOCK
        kn, vn = k[:, :t1], v[:, :t1]
        z = jnp.einsum('bqhd,bshd->bhqs', q[:, t0:t1], kn).astype(jnp.float32) / math.sqrt(D)
        causal = jnp.arange(t1)[None, :] < (t0 + jnp.arange(BLOCK))[:, None]
        log_beta = jax.nn.log_sigmoid(z)
        log_1m = jnp.where(causal, jax.nn.log_sigmoid(-z), 0.0)
        between = lax.cumsum(log_1m, axis=log_1m.ndim - 1, reverse=True) - log_1m
        a = jnp.where(causal, jnp.exp(log_beta + between), 0.0)
        outs.append(jnp.einsum('bhqs,bshd->bqhd', a.astype(v.dtype), vn))
    return jnp.concatenate(outs, axis=1)


def memory_cross_attention(h, mem, w_q, w_kv, w_o):
    B, S, _ = h.shape
    q = (h @ w_q).reshape(B, S, N_HEADS_MEM, HEAD_DIM_MEM)
    k, v = jnp.split(mem @ w_kv, 2, axis=-1)
    k = k.reshape(B, -1, N_HEADS_MEM, HEAD_DIM_MEM)
    v = v.reshape(B, -1, N_HEADS_MEM, HEAD_DIM_MEM)
    s = jnp.einsum('bqhd,bmhd->bhqm', q, k).astype(jnp.float32) / math.sqrt(HEAD_DIM_MEM)
    p = jax.nn.softmax(s, axis=-1).astype(v.dtype)
    o = jnp.einsum('bhqm,bmhd->bqhd', p, v).reshape(B, S, D_MODEL)
    return o @ w_o


def setup_inputs(seed: int = 0) -> dict:
    key = jax.random.key(seed)
    ks = jax.random.split(key, 20)
    f32 = jnp.float32

    def nrm(k, shape, scale):
        return jax.random.normal(k, shape, f32) * scale

    d = D_MODEL
    col_scale = jnp.concatenate([
        jnp.ones((D_SWA + D_KV_SWA,), f32), jnp.full((D_KV_SWA,), BETA, f32),
        jnp.ones((2 * D_SB,), f32), jnp.full((D_SB,), BETA, f32)])
    kv_scale = jnp.concatenate([jnp.ones((d,), f32), jnp.full((d,), BETA, f32)])
    return {
        "x": jax.random.normal(ks[0], (BATCH, SEQ, d), f32),
        "mem": jax.random.normal(ks[1], (BATCH, MEM_LEN, d), f32),
        "w_in": nrm(ks[2], (DEPTH, d, D_IN), d ** -0.5) * col_scale,
        "sinks": nrm(ks[3], (DEPTH, N_HEADS_SWA), 0.5),
        "g_swa": 1.0 + nrm(ks[4], (DEPTH, D_SWA), 0.02),
        "g_sb": 1.0 + nrm(ks[5], (DEPTH, D_SB), 0.02),
        "w_o": nrm(ks[6], (DEPTH, D_MIX, d), BETA * D_MIX ** -0.5),
        "ln1_g": 1.0 + nrm(ks[7], (DEPTH, d), 0.02),
        "ln1_b": nrm(ks[8], (DEPTH, d), 0.02),
        "w_q_mem": nrm(ks[9], (DEPTH, d, d), d ** -0.5),
        "w_kv_mem": nrm(ks[10], (DEPTH, d, 2 * d), d ** -0.5) * kv_scale,
        "w_o_mem": nrm(ks[11], (DEPTH, d, d), BETA * d ** -0.5),
        "ln2_g": 1.0 + nrm(ks[12], (DEPTH, d), 0.02),
        "ln2_b": nrm(ks[13], (DEPTH, d), 0.02),
        "w_gate_up": nrm(ks[14], (DEPTH, d, 2 * D_FF), d ** -0.5),
        "w_down": nrm(ks[15], (DEPTH, D_FF, d), BETA * D_FF ** -0.5),
        "ln3_g": 1.0 + nrm(ks[16], (DEPTH, d), 0.02),
        "ln3_b": nrm(ks[17], (DEPTH, d), 0.02),
    }


def reference(x, mem, w_in, sinks, g_swa, g_sb, w_o, ln1_g, ln1_b, w_q_mem, w_kv_mem,
              w_o_mem, ln2_g, ln2_b, w_gate_up, w_down, ln3_g, ln3_b):
    B, S, _ = x.shape
    splits = np.cumsum([D_SWA, D_KV_SWA, D_KV_SWA, D_SB, D_SB]).tolist()
    h = x
    for l in range(DEPTH):
        q_a, k_a, v_a, q_b, k_b, v_b = jnp.split(h @ w_in[l], splits, axis=-1)
        o_a = swa_sink_attention(
            q_a.reshape(B, S, N_HEADS_SWA, HEAD_DIM),
            k_a.reshape(B, S, N_KV_SWA, HEAD_DIM),
            v_a.reshape(B, S, N_KV_SWA, HEAD_DIM), sinks[l])
        o_b = stick_breaking_attention(
            q_b.reshape(B, S, N_HEADS_SB, HEAD_DIM),
            k_b.reshape(B, S, N_HEADS_SB, HEAD_DIM),
            v_b.reshape(B, S, N_HEADS_SB, HEAD_DIM))
        o_a = head_rmsnorm(o_a, g_swa[l]).reshape(B, S, D_SWA)
        o_b = head_rmsnorm(o_b, g_sb[l]).reshape(B, S, D_SB)
        mix = jnp.concatenate([o_a, o_b], axis=-1) @ w_o[l]
        h = layer_norm(ALPHA * h + mix, ln1_g[l], ln1_b[l])
        c = memory_cross_attention(h, mem, w_q_mem[l], w_kv_mem[l], w_o_mem[l])
        h = layer_norm(ALPHA * h + c, ln2_g[l], ln2_b[l])
        gate, up = jnp.split(h @ w_gate_up[l], 2, axis=-1)
        f = (jax.nn.silu(gate) * up) @ w_down[l]
        h = layer_norm(ALPHA * h + f, ln3_g[l], ln3_b[l])
    return h
```

```python
import functools
import math

import jax
import jax.numpy as jnp
from jax import lax
from jax.experimental import pallas as pl
from jax.experimental.pallas import tpu as pltpu

F32 = jnp.float32
BF16 = jnp.bfloat16

D_MODEL = 2048
HEAD_DIM = 64
N_HEADS_SWA = 16
N_KV_SWA = 4
N_HEADS_SB = 16
WINDOW = 128
BLOCK = 128
N_HEADS_MEM = 4
HEAD_DIM_MEM = D_MODEL // N_HEADS_MEM
D_SWA = N_HEADS_SWA * HEAD_DIM
D_KV_SWA = N_KV_SWA * HEAD_DIM
D_SB = N_HEADS_SB * HEAD_DIM
D_IN = D_SWA + 2 * D_KV_SWA + 3 * D_SB
DEPTH = 1
ALPHA = (2.0 * DEPTH) ** 0.25
LN_EPS = 1e-5
RMS_EPS = 1e-6
QK_SCALE = 1.0 / math.sqrt(HEAD_DIM)

LANES = 128
V7X_VMEM_LIMIT = 56 * 1024 * 1024

_KA_BLK = D_SWA // LANES
_VA_BLK = (D_SWA + D_KV_SWA) // LANES
_QB_BLK = (D_SWA + 2 * D_KV_SWA) // LANES
_KB_BLK = _QB_BLK + D_SB // LANES
_VB_BLK = _KB_BLK + D_SB // LANES


def _params(sem, vmem_bytes):
    return pltpu.CompilerParams(
        dimension_semantics=sem, vmem_limit_bytes=min(int(vmem_bytes), V7X_VMEM_LIMIT))


def _dot(a, b):
    return jnp.dot(a, b, preferred_element_type=F32)


def _dot_nt(a, b):
    return lax.dot_general(a, b, (((1,), (1,)), ((), ())), preferred_element_type=F32)


def _layer_norm(y, g, b):
    mu = jnp.mean(y, axis=-1, keepdims=True)
    d = y - mu
    var = jnp.mean(d * d, axis=-1, keepdims=True)
    return d * lax.rsqrt(var + LN_EPS) * g + b


def _mm_kernel(x_ref, w_ref, o_ref):
    o_ref[...] = _dot(x_ref[...], w_ref[...]).astype(o_ref.dtype)


def _matmul(x, w, bm, bn, out_dtype, name):
    m, k = x.shape
    n = w.shape[1]
    assert m % bm == 0 and n % bn == 0
    osz = jnp.dtype(out_dtype).itemsize
    vmem = 2 * (bm * k * 2 + k * bn * 2 + bm * bn * osz) + bm * bn * 4
    return pl.pallas_call(
        _mm_kernel,
        grid=(m // bm, n // bn),
        in_specs=[pl.BlockSpec((bm, k), lambda i, j: (i, 0)),
                  pl.BlockSpec((k, bn), lambda i, j: (0, j))],
        out_specs=pl.BlockSpec((bm, bn), lambda i, j: (i, j)),
        out_shape=jax.ShapeDtypeStruct((m, n), out_dtype),
        compiler_params=_params(("parallel", "parallel"), vmem + (8 << 20)),
        name=name,
    )(x, w)


def _alibi_slope(h):
    return 2.0 ** (-8.0 * (h + 1) / N_HEADS_SWA)


def _swa_kernel(sink_ref, q_ref, kp_ref, kc_ref, vp_ref, vc_ref, g_ref, o_ref):
    kvp = pl.program_id(1)
    n = pl.program_id(2)
    kcat = jnp.concatenate([kp_ref[...], kc_ref[...]], axis=0)
    vcat = jnp.concatenate([vp_ref[...], vc_ref[...]], axis=0)
    qi = lax.broadcasted_iota(jnp.int32, (BLOCK, 2 * BLOCK), 0)
    kj = lax.broadcasted_iota(jnp.int32, (BLOCK, 2 * BLOCK), 1)
    dist = qi + BLOCK - kj
    valid = (dist >= 0) & (dist < WINDOW) & ((kj >= BLOCK) | (n > 0))
    pen = jnp.where(valid, dist.astype(F32), jnp.inf)
    lane = lax.broadcasted_iota(jnp.int32, (BLOCK, LANES), 1)
    halves = [lane < HEAD_DIM, lane >= HEAD_DIM]

    for t in range(4):
        c = t // 2
        q_t = q_ref[:, t * LANES:(t + 1) * LANES]
        q_sw = pltpu.roll(q_t.astype(F32), HEAD_DIM, 1).astype(BF16)
        ys = []
        for u in range(2):
            src = q_t if u == c else q_sw
            qm = jnp.where(halves[c], src, jnp.zeros_like(src)) * QK_SCALE
            s = _dot_nt(qm, kcat)
            slope = jnp.where(kvp == 0, _alibi_slope(t * 2 + u), _alibi_slope(8 + t * 2 + u))
            s = s - slope * pen
            sink = sink_ref[kvp * 8 + t * 2 + u]
            m = jnp.maximum(jnp.max(s, axis=-1, keepdims=True), sink)
            p = jnp.exp(s - m)
            denom = jnp.sum(p, axis=-1, keepdims=True) + jnp.exp(sink - m)
            o = _dot(p.astype(BF16), vcat) / denom
            ms = jnp.sum(jnp.where(halves[c], o * o, 0.0), axis=-1, keepdims=True) / HEAD_DIM
            y = o * lax.rsqrt(ms + RMS_EPS)
            if u != c:
                y = pltpu.roll(y, HEAD_DIM, 1)
            ys.append(y)
        y2 = jnp.where(halves[0], ys[0], ys[1])
        o_ref[:, t * LANES:(t + 1) * LANES] = (
            y2 * g_ref[:, t * LANES:(t + 1) * LANES]).astype(o_ref.dtype)


def _swa_attention(qkv, sinks, g_swa):
    b, s, _ = qkv.shape
    nb = s // BLOCK
    qw = 4 * LANES
    kv_spec = lambda blk, prev: pl.BlockSpec(
        (None, BLOCK, LANES),
        (lambda bi, kp, n: (bi, jnp.maximum(n - 1, 0), blk + kp)) if prev
        else (lambda bi, kp, n: (bi, n, blk + kp)))
    return pl.pallas_call(
        _swa_kernel,
        grid=(b, 2, nb),
        in_specs=[pl.BlockSpec(memory_space=pltpu.SMEM),
                  pl.BlockSpec((None, BLOCK, qw), lambda bi, kp, n: (bi, n, kp)),
                  kv_spec(_KA_BLK, True), kv_spec(_KA_BLK, False),
                  kv_spec(_VA_BLK, True), kv_spec(_VA_BLK, False),
                  pl.BlockSpec((1, qw), lambda bi, kp, n: (0, kp))],
        out_specs=pl.BlockSpec((None, BLOCK, qw), lambda bi, kp, n: (bi, n, kp)),
        out_shape=jax.ShapeDtypeStruct((b, s, D_SWA), BF16),
        compiler_params=_params(("parallel", "parallel", "parallel"), 16 << 20),
        name="swa_attention",
    )(sinks, qkv, qkv, qkv, qkv, qkv, g_swa)


_SB_TILE = 256


def _sb_kernel(q_ref, k_ref, v_ref, g_ref, o_ref):
    qb = pl.program_id(2)
    tq = tk = _SB_TILE
    lane = lax.broadcasted_iota(jnp.int32, (tq, LANES), 1)
    halves = [lane < HEAD_DIM, lane >= HEAD_DIM]
    q = q_ref[...] * QK_SCALE
    qms = [jnp.where(hm, q, jnp.zeros_like(q)) for hm in halves]
    r = lax.broadcasted_iota(jnp.int32, (tk, tk), 0)
    c = lax.broadcasted_iota(jnp.int32, (tk, tk), 1)
    upper = (r > c).astype(BF16)
    causal = c < r

    def tile(j, state, diag):
        start = pl.multiple_of(j * tk, tk)
        k = k_ref[pl.ds(start, tk), :]
        v = v_ref[pl.ds(start, tk), :]
        new = []
        for h in range(2):
            acc, carry = state[h]
            z = _dot_nt(qms[h], k)
            sp = jnp.log1p(jnp.exp(-jnp.abs(z)))
            log_beta = jnp.minimum(z, 0.0) - sp
            log_1m = log_beta - z
            if diag:
                log_1m = jnp.where(causal, log_1m, 0.0)
            hi = log_1m.astype(BF16)
            lo = (log_1m - hi.astype(F32)).astype(BF16)
            between = _dot(hi, upper) + _dot(lo, upper) + carry
            a = jnp.exp(log_beta + between)
            if diag:
                a = jnp.where(causal, a, 0.0)
            acc = acc + _dot(a.astype(BF16), v)
            carry = carry + jnp.sum(log_1m, axis=-1, keepdims=True)
            new.append((acc, carry))
        return tuple(new)

    zero = (jnp.zeros((tq, LANES), F32), jnp.zeros((tq, 1), F32))
    state = tile(qb, (zero, zero), True)
    state = lax.fori_loop(0, qb, lambda i, st: tile(qb - 1 - i, st, False), state)

    ys = []
    for h in range(2):
        o = state[h][0]
        ms = jnp.sum(jnp.where(halves[h], o * o, 0.0), axis=-1, keepdims=True) / HEAD_DIM
        ys.append(o * lax.rsqrt(ms + RMS_EPS))
    o_ref[...] = (jnp.where(halves[0], ys[0], ys[1]) * g_ref[...]).astype(o_ref.dtype)


def _sb_attention(qkv, g_sb):
    b, s, _ = qkv.shape
    npair = D_SB // LANES
    return pl.pallas_call(
        _sb_kernel,
        grid=(b, npair, s // _SB_TILE),
        in_specs=[pl.BlockSpec((None, _SB_TILE, LANES), lambda bi, p, i: (bi, i, _QB_BLK + p)),
                  pl.BlockSpec((None, s, LANES), lambda bi, p, i: (bi, 0, _KB_BLK + p)),
                  pl.BlockSpec((None, s, LANES), lambda bi, p, i: (bi, 0, _VB_BLK + p)),
                  pl.BlockSpec((1, LANES), lambda bi, p, i: (0, p))],
        out_specs=pl.BlockSpec((None, _SB_TILE, LANES), lambda bi, p, i: (bi, i, p)),
        out_shape=jax.ShapeDtypeStruct((b, s, D_SB), BF16),
        compiler_params=_params(("parallel", "parallel", "parallel"), 32 << 20),
        name="sb_attention",
    )(qkv, qkv, qkv, g_sb)


def _proj_ln_kernel(oa_ref, ob_ref, w_ref, x_ref, g_ref, b_ref, h_ref, hb_ref):
    mix = _dot(oa_ref[...], w_ref[:D_SWA, :]) + _dot(ob_ref[...], w_ref[D_SWA:, :])
    h = _layer_norm(ALPHA * x_ref[...] + mix, g_ref[...], b_ref[...])
    h_ref[...] = h
    hb_ref[...] = h.astype(BF16)


def _proj_ln(o_a, o_b, w_o, x, g, b, bm):
    m, d = x.shape
    row = lambda w: pl.BlockSpec((bm, w), lambda i: (i, 0))
    full = lambda r, w: pl.BlockSpec((r, w), lambda i: (0, 0))
    vmem = 2 * (2 * bm * D_SWA * 2 + d * d * 2 + bm * d * (4 + 4 + 2)) + 2 * bm * d * 4
    return pl.pallas_call(
        _proj_ln_kernel,
        grid=(m // bm,),
        in_specs=[row(D_SWA), row(D_SB), full(d, d), row(d), full(1, d), full(1, d)],
        out_specs=[row(d), row(d)],
        out_shape=[jax.ShapeDtypeStruct((m, d), F32), jax.ShapeDtypeStruct((m, d), BF16)],
        compiler_params=_params(("parallel",), vmem + (4 << 20)),
        name="mix_proj_ln",
    )(o_a, o_b, w_o, x, g, b)


def _xattn_kernel(q_ref, kv_ref, w_ref, h_ref, g_ref, b_ref, o_ref, ob_ref):
    scale = 1.0 / math.sqrt(HEAD_DIM_MEM)
    outs = []
    for hd in range(N_HEADS_MEM):
        lo, hi = hd * HEAD_DIM_MEM, (hd + 1) * HEAD_DIM_MEM
        s = _dot_nt(q_ref[:, lo:hi], kv_ref[:, lo:hi]) * scale
        p = jnp.exp(s - jnp.max(s, axis=-1, keepdims=True))
        l = jnp.sum(p, axis=-1, keepdims=True)
        o = _dot(p.astype(BF16), kv_ref[:, D_MODEL + lo:D_MODEL + hi]) / l
        outs.append(o.astype(BF16))
    c = _dot(jnp.concatenate(outs, axis=-1), w_ref[...])
    h = _layer_norm(ALPHA * h_ref[...] + c, g_ref[...], b_ref[...])
    o_ref[...] = h
    ob_ref[...] = h.astype(BF16)


def _xattn_ln(q, kv, w_o, h, g, b, bm):
    bsz, s, d = q.shape
    mlen = kv.shape[1]
    row = lambda: pl.BlockSpec((None, bm, d), lambda bi, i: (bi, i, 0))
    full = lambda r, w: pl.BlockSpec((r, w), lambda bi, i: (0, 0))
    vmem = 2 * (bm * d * 2 + mlen * 2 * d * 2 + d * d * 2 + bm * d * (4 + 4 + 2)) + 3 * bm * d * 4
    return pl.pallas_call(
        _xattn_kernel,
        grid=(bsz, s // bm),
        in_specs=[row(), pl.BlockSpec((None, mlen, 2 * d), lambda bi, i: (bi, 0, 0)),
                  full(d, d), row(), full(1, d), full(1, d)],
        out_specs=[row(), row()],
        out_shape=[jax.ShapeDtypeStruct((bsz, s, d), F32),
                   jax.ShapeDtypeStruct((bsz, s, d), BF16)],
        compiler_params=_params(("parallel", "parallel"), vmem + (4 << 20)),
        name="mem_xattn_ln",
    )(q, kv, w_o, h, g, b)


def _ffn_kernel(xb_ref, wg_ref, wu_ref, wd_ref, h_ref, g_ref, b_ref, o_ref, acc_ref):
    j = pl.program_id(1)

    @pl.when(j == 0)
    def _():
        acc_ref[...] = jnp.zeros_like(acc_ref)

    x = xb_ref[...]
    gate = _dot(x, wg_ref[...])
    up = _dot(x, wu_ref[...])
    act = gate / (1.0 + jnp.exp(-gate)) * up
    acc_ref[...] += _dot(act.astype(BF16), wd_ref[...])

    @pl.when(j == pl.num_programs(1) - 1)
    def _():
        o_ref[...] = _layer_norm(ALPHA * h_ref[...] + acc_ref[...], g_ref[...], b_ref[...])


def _ffn_ln(hb, w_gu, w_d, h, g, b, bm, bf):
    m, d = h.shape
    dff = w_d.shape[0]
    nf = dff // bf
    assert dff % bf == 0 and m % bm == 0
    row = lambda: pl.BlockSpec((bm, d), lambda i, j: (i, 0))
    full = lambda: pl.BlockSpec((1, d), lambda i, j: (0, 0))
    vmem = 2 * (bm * d * 2 + 3 * d * bf * 2 + 2 * bm * d * 4) + bm * d * 4 + 4 * bm * bf * 4
    return pl.pallas_call(
        _ffn_kernel,
        grid=(m // bm, nf),
        in_specs=[row(),
                  pl.BlockSpec((d, bf), lambda i, j: (0, j)),
                  pl.BlockSpec((d, bf), lambda i, j: (0, j + nf)),
                  pl.BlockSpec((bf, d), lambda i, j: (j, 0)),
                  row(), full(), full()],
        out_specs=row(),
        out_shape=jax.ShapeDtypeStruct((m, d), F32),
        scratch_shapes=[pltpu.VMEM((bm, d), F32)],
        compiler_params=_params(("parallel", "arbitrary"), vmem + (4 << 20)),
        name="swiglu_ffn_ln",
    )(hb, w_gu, w_gu, w_d, h, g, b)


def kernel(x, mem, w_in, sinks, g_swa, g_sb, w_o, ln1_g, ln1_b, w_q_mem, w_kv_mem,
           w_o_mem, ln2_g, ln2_b, w_gate_up, w_down, ln3_g, ln3_b):
    bsz, s, d = x.shape
    t = bsz * s
    mlen = mem.shape[1]
    h = x.reshape(t, d)
    for l in range(DEPTH):
        row = lambda a: a[l].reshape(1, -1)
        hb = h.astype(BF16)
        qkv = _matmul(hb, w_in[l].astype(BF16), 1024, 1152, BF16, "qkv_proj")
        qkv = qkv.reshape(bsz, s, D_IN)
        o_a = _swa_attention(qkv, sinks[l], row(g_swa))
        o_b = _sb_attention(qkv, row(g_sb))
        h, hb = _proj_ln(o_a.reshape(t, D_SWA), o_b.reshape(t, D_SB), w_o[l].astype(BF16),
                         h, row(ln1_g), row(ln1_b), 512)
        q = _matmul(hb, w_q_mem[l].astype(BF16), 1024, 1024, BF16, "mem_q_proj")
        kv = _matmul(mem.reshape(bsz * mlen, d).astype(BF16), w_kv_mem[l].astype(BF16),
                     1024, 1024, BF16, "mem_kv_proj")
        h3, hb3 = _xattn_ln(q.reshape(bsz, s, d), kv.reshape(bsz, mlen, 2 * d),
                            w_o_mem[l].astype(BF16), h.reshape(bsz, s, d),
                            row(ln2_g), row(ln2_b), 256)
        h, hb = h3.reshape(t, d), hb3.reshape(t, d)
        h = _ffn_ln(hb, w_gate_up[l].astype(BF16), w_down[l].astype(BF16), h,
                    row(ln3_g), row(ln3_b), 512, 512)
    return h.reshape(bsz, s, d)
```

```python
import functools
import math

import jax
import jax.numpy as jnp
from jax import lax
from jax.experimental import pallas as pl
from jax.experimental.pallas import tpu as pltpu

F32 = jnp.float32
BF16 = jnp.bfloat16

D_MODEL = 2048
HEAD_DIM = 64
N_HEADS_SWA = 16
N_KV_SWA = 4
N_HEADS_SB = 16
WINDOW = 128
BLOCK = 128
N_HEADS_MEM = 4
HEAD_DIM_MEM = D_MODEL // N_HEADS_MEM
D_SWA = N_HEADS_SWA * HEAD_DIM
D_KV_SWA = N_KV_SWA * HEAD_DIM
D_SB = N_HEADS_SB * HEAD_DIM
D_IN = D_SWA + 2 * D_KV_SWA + 3 * D_SB
DEPTH = 1
ALPHA = (2.0 * DEPTH) ** 0.25
LN_EPS = 1e-5
RMS_EPS = 1e-6
QK_SCALE = 1.0 / math.sqrt(HEAD_DIM)

LANES = 128
V7X_VMEM_LIMIT = 56 * 1024 * 1024

_KA_BLK = D_SWA // LANES
_VA_BLK = (D_SWA + D_KV_SWA) // LANES
_QB_BLK = (D_SWA + 2 * D_KV_SWA) // LANES
_KB_BLK = _QB_BLK + D_SB // LANES
_VB_BLK = _KB_BLK + D_SB // LANES


def _params(sem, vmem_bytes):
    return pltpu.CompilerParams(
        dimension_semantics=sem, vmem_limit_bytes=min(int(vmem_bytes), V7X_VMEM_LIMIT))


def _dot(a, b):
    return jnp.dot(a, b, preferred_element_type=F32)


def _dot_nt(a, b):
    return lax.dot_general(a, b, (((1,), (1,)), ((), ())), preferred_element_type=F32)


def _layer_norm(y, g, b):
    mu = jnp.mean(y, axis=-1, keepdims=True)
    d = y - mu
    var = jnp.mean(d * d, axis=-1, keepdims=True)
    return d * lax.rsqrt(var + LN_EPS) * g + b


def _mm_kernel(x_ref, w_ref, o_ref):
    o_ref[...] = _dot(x_ref[...], w_ref[...]).astype(o_ref.dtype)


def _matmul(x, w, bm, bn, out_dtype, name):
    m, k = x.shape
    n = w.shape[1]
    assert m % bm == 0 and n % bn == 0
    osz = jnp.dtype(out_dtype).itemsize
    vmem = 2 * (bm * k * 2 + k * bn * 2 + bm * bn * osz) + bm * bn * 4
    return pl.pallas_call(
        _mm_kernel,
        grid=(m // bm, n // bn),
        in_specs=[pl.BlockSpec((bm, k), lambda i, j: (i, 0)),
                  pl.BlockSpec((k, bn), lambda i, j: (0, j))],
        out_specs=pl.BlockSpec((bm, bn), lambda i, j: (i, j)),
        out_shape=jax.ShapeDtypeStruct((m, n), out_dtype),
        compiler_params=_params(("parallel", "parallel"), vmem + (8 << 20)),
        name=name,
    )(x, w)


def _alibi_slope(h):
    return 2.0 ** (-8.0 * (h + 1) / N_HEADS_SWA)


def _swa_kernel(sink_ref, q_ref, kp_ref, kc_ref, vp_ref, vc_ref, g_ref, o_ref):
    kvp = pl.program_id(1)
    n = pl.program_id(2)
    kcat = jnp.concatenate([kp_ref[...], kc_ref[...]], axis=0)
    vcat = jnp.concatenate([vp_ref[...], vc_ref[...]], axis=0)
    qi = lax.broadcasted_iota(jnp.int32, (BLOCK, 2 * BLOCK), 0)
    kj = lax.broadcasted_iota(jnp.int32, (BLOCK, 2 * BLOCK), 1)
    dist = qi + BLOCK - kj
    valid = (dist >= 0) & (dist < WINDOW) & ((kj >= BLOCK) | (n > 0))
    pen = jnp.where(valid, dist.astype(F32), jnp.inf)
    lane = lax.broadcasted_iota(jnp.int32, (BLOCK, LANES), 1)
    halves = [lane < HEAD_DIM, lane >= HEAD_DIM]

    for t in range(4):
        c = t // 2
        q_t = q_ref[:, t * LANES:(t + 1) * LANES]
        q_sw = pltpu.roll(q_t.astype(F32), HEAD_DIM, 1).astype(BF16)
        ys = []
        for u in range(2):
            src = q_t if u == c else q_sw
            qm = jnp.where(halves[c], src, jnp.zeros_like(src)) * QK_SCALE
            s = _dot_nt(qm, kcat)
            slope = jnp.where(kvp == 0, _alibi_slope(t * 2 + u), _alibi_slope(8 + t * 2 + u))
            s = s - slope * pen
            sink = sink_ref[kvp * 8 + t * 2 + u]
            m = jnp.maximum(jnp.max(s, axis=-1, keepdims=True), sink)
            p = jnp.exp(s - m)
            denom = jnp.sum(p, axis=-1, keepdims=True) + jnp.exp(sink - m)
            o = _dot(p.astype(BF16), vcat) / denom
            ms = jnp.sum(jnp.where(halves[c], o * o, 0.0), axis=-1, keepdims=True) / HEAD_DIM
            y = o * lax.rsqrt(ms + RMS_EPS)
            if u != c:
                y = pltpu.roll(y, HEAD_DIM, 1)
            ys.append(y)
        y2 = jnp.where(halves[0], ys[0], ys[1])
        o_ref[:, t * LANES:(t + 1) * LANES] = (
            y2 * g_ref[:, t * LANES:(t + 1) * LANES]).astype(o_ref.dtype)


def _swa_attention(qkv, sinks, g_swa):
    b, s, _ = qkv.shape
    nb = s // BLOCK
    qw = 4 * LANES
    kv_spec = lambda blk, prev: pl.BlockSpec(
        (None, BLOCK, LANES),
        (lambda bi, kp, n: (bi, jnp.maximum(n - 1, 0), blk + kp)) if prev
        else (lambda bi, kp, n: (bi, n, blk + kp)))
    return pl.pallas_call(
        _swa_kernel,
        grid=(b, 2, nb),
        in_specs=[pl.BlockSpec(memory_space=pltpu.SMEM),
                  pl.BlockSpec((None, BLOCK, qw), lambda bi, kp, n: (bi, n, kp)),
                  kv_spec(_KA_BLK, True), kv_spec(_KA_BLK, False),
                  kv_spec(_VA_BLK, True), kv_spec(_VA_BLK, False),
                  pl.BlockSpec((1, qw), lambda bi, kp, n: (0, kp))],
        out_specs=pl.BlockSpec((None, BLOCK, qw), lambda bi, kp, n: (bi, n, kp)),
        out_shape=jax.ShapeDtypeStruct((b, s, D_SWA), BF16),
        compiler_params=_params(("parallel", "parallel", "parallel"), 16 << 20),
        name="swa_attention",
    )(sinks, qkv, qkv, qkv, qkv, qkv, g_swa)


_SB_TK = 256
_SB_TQ = 512
_SB_CHUNK = 32
_SB_MASKED = -1e30
_LOG2E = 1.4426950408889634


def _sb_kernel(q_ref, k_ref, v_ref, nu_ref, g_ref, o_ref,
               qm_ref, z_ref, hl_ref, bt_ref, a_ref, acc_ref, car_ref):
    qb = pl.program_id(2)
    tq, tk, ch = _SB_TQ, _SB_TK, _SB_CHUNK
    ntile = tq // tk
    last = qb * ntile + ntile - 1
    lane = lax.broadcasted_iota(jnp.int32, (tq, LANES), 1)
    q = q_ref[...] * QK_SCALE
    qm_ref[0] = jnp.where(lane < HEAD_DIM, q, jnp.zeros_like(q))
    qm_ref[1] = jnp.where(lane >= HEAD_DIM, q, jnp.zeros_like(q))
    acc_ref[...] = jnp.zeros_like(acc_ref)
    car_ref[...] = jnp.zeros_like(car_ref)
    a_ref[...] = jnp.zeros_like(a_ref)
    cmr = (lax.broadcasted_iota(jnp.int32, (ch, tk), 1)
           - lax.broadcasted_iota(jnp.int32, (ch, tk), 0))

    def key_tile(ref, j):
        return ref[pl.ds(pl.multiple_of(j * tk, tk), tk), :]

    def z_stage(slot, j):
        k = key_tile(k_ref, j)
        for h in range(2):
            z_ref[slot, h] = _dot_nt(qm_ref[h], k)

    def p_stage(slot, h, off):
        for r0 in range(0, tq, ch):
            rows = slice(r0, r0 + ch)
            t = z_ref[slot, h, rows, :] * _LOG2E
            p = jnp.maximum(t, 0.0) + jnp.log2(1.0 + jnp.exp2(-jnp.abs(t)))
            car = car_ref[h, rows, :]
            tc = t + jnp.concatenate([car, car], axis=1)
            if off is not None:
                causal = cmr < off + r0
                p = jnp.where(causal, p, 0.0)
                tc = jnp.where(causal, tc, _SB_MASKED)
            hi = p.astype(BF16)
            lo = (p - hi.astype(F32)).astype(BF16)
            hl_ref[h, rows, :tk] = hi
            hl_ref[h, rows, tk:] = lo
            z_ref[slot, h, rows, :] = tc
            car_ref[h, rows, :] = car - jnp.sum(p, axis=-1, keepdims=True)

    def w_stage(slot, h):
        for r0 in range(0, tq, ch):
            rows = slice(r0, r0 + ch)
            a_ref[h, rows, :] = jnp.exp2(z_ref[slot, h, rows, :] + bt_ref[h, rows, :]).astype(BF16)

    def pv_stage(j):
        v = key_tile(v_ref, j)
        for h in range(2):
            acc_ref[h] += _dot(a_ref[h], v)

    def body(j, slot, off):
        pv_stage(jnp.minimum(j + 1, last))
        p_stage(slot, 0, off)
        bt_ref[0] = _dot(hl_ref[0], nu_ref[...])
        p_stage(slot, 1, off)
        bt_ref[1] = _dot(hl_ref[1], nu_ref[...])
        z_stage(1 - slot, jnp.maximum(j - 1, 0))
        w_stage(slot, 0)
        w_stage(slot, 1)

    z_stage(0, last)

    assert ntile % 2 == 0
    for i in range(ntile):
        body(last - i, i % 2, (i + 1 - ntile) * tk)

    @pl.loop(0, qb * (ntile // 2))
    def _(m):
        j = qb * ntile - 1 - 2 * m
        body(j, 0, None)
        body(j - 1, 1, None)

    pv_stage(0)

    ys = []
    for h, half in enumerate([lane < HEAD_DIM, lane >= HEAD_DIM]):
        o = acc_ref[h]
        ms = jnp.sum(jnp.where(half, o * o, 0.0), axis=-1, keepdims=True) / HEAD_DIM
        ys.append(o * lax.rsqrt(ms + RMS_EPS))
    o_ref[...] = (jnp.where(lane < HEAD_DIM, ys[0], ys[1]) * g_ref[...]).astype(o_ref.dtype)


def _sb_attention(qkv, g_sb):
    b, s, _ = qkv.shape
    npair = D_SB // LANES
    tq, tk = _SB_TQ, _SB_TK
    r = lax.broadcasted_iota(jnp.int32, (2 * tk, tk), 0) % tk
    c = lax.broadcasted_iota(jnp.int32, (2 * tk, tk), 1)
    nu = -(r >= c).astype(BF16)
    return pl.pallas_call(
        _sb_kernel,
        grid=(b, npair, s // tq),
        in_specs=[pl.BlockSpec((None, tq, LANES), lambda bi, p, i: (bi, i, _QB_BLK + p)),
                  pl.BlockSpec((None, s, LANES), lambda bi, p, i: (bi, 0, _KB_BLK + p)),
                  pl.BlockSpec((None, s, LANES), lambda bi, p, i: (bi, 0, _VB_BLK + p)),
                  pl.BlockSpec((2 * tk, tk), lambda bi, p, i: (0, 0)),
                  pl.BlockSpec((1, LANES), lambda bi, p, i: (0, p))],
        out_specs=pl.BlockSpec((None, tq, LANES), lambda bi, p, i: (bi, i, p)),
        out_shape=jax.ShapeDtypeStruct((b, s, D_SB), BF16),
        scratch_shapes=[pltpu.VMEM((2, tq, LANES), BF16),
                        pltpu.VMEM((2, 2, tq, tk), F32),
                        pltpu.VMEM((2, tq, 2 * tk), BF16),
                        pltpu.VMEM((2, tq, tk), F32),
                        pltpu.VMEM((2, tq, tk), BF16),
                        pltpu.VMEM((2, tq, LANES), F32),
                        pltpu.VMEM((2, tq, LANES), F32)],
        compiler_params=_params(("parallel", "parallel", "parallel"), 32 << 20),
        name="sb_attention",
    )(qkv, qkv, qkv, nu, g_sb)


def _proj_ln_kernel(oa_ref, ob_ref, w_ref, x_ref, g_ref, b_ref, h_ref, hb_ref):
    mix = _dot(oa_ref[...], w_ref[:D_SWA, :]) + _dot(ob_ref[...], w_ref[D_SWA:, :])
    h = _layer_norm(ALPHA * x_ref[...] + mix, g_ref[...], b_ref[...])
    h_ref[...] = h
    hb_ref[...] = h.astype(BF16)


def _proj_ln(o_a, o_b, w_o, x, g, b, bm):
    m, d = x.shape
    row = lambda w: pl.BlockSpec((bm, w), lambda i: (i, 0))
    full = lambda r, w: pl.BlockSpec((r, w), lambda i: (0, 0))
    vmem = 2 * (2 * bm * D_SWA * 2 + d * d * 2 + bm * d * (4 + 4 + 2)) + 2 * bm * d * 4
    return pl.pallas_call(
        _proj_ln_kernel,
        grid=(m // bm,),
        in_specs=[row(D_SWA), row(D_SB), full(d, d), row(d), full(1, d), full(1, d)],
        out_specs=[row(d), row(d)],
        out_shape=[jax.ShapeDtypeStruct((m, d), F32), jax.ShapeDtypeStruct((m, d), BF16)],
        compiler_params=_params(("parallel",), vmem + (4 << 20)),
        name="mix_proj_ln",
    )(o_a, o_b, w_o, x, g, b)


def _xattn_kernel(q_ref, kv_ref, w_ref, h_ref, g_ref, b_ref, o_ref, ob_ref):
    scale = 1.0 / math.sqrt(HEAD_DIM_MEM)
    outs = []
    for hd in range(N_HEADS_MEM):
        lo, hi = hd * HEAD_DIM_MEM, (hd + 1) * HEAD_DIM_MEM
        s = _dot_nt(q_ref[:, lo:hi], kv_ref[:, lo:hi]) * scale
        p = jnp.exp(s - jnp.max(s, axis=-1, keepdims=True))
        l = jnp.sum(p, axis=-1, keepdims=True)
        o = _dot(p.astype(BF16), kv_ref[:, D_MODEL + lo:D_MODEL + hi]) / l
        outs.append(o.astype(BF16))
    c = _dot(jnp.concatenate(outs, axis=-1), w_ref[...])
    h = _layer_norm(ALPHA * h_ref[...] + c, g_ref[...], b_ref[...])
    o_ref[...] = h
    ob_ref[...] = h.astype(BF16)


def _xattn_ln(q, kv, w_o, h, g, b, bm):
    bsz, s, d = q.shape
    mlen = kv.shape[1]
    row = lambda: pl.BlockSpec((None, bm, d), lambda bi, i: (bi, i, 0))
    full = lambda r, w: pl.BlockSpec((r, w), lambda bi, i: (0, 0))
    vmem = 2 * (bm * d * 2 + mlen * 2 * d * 2 + d * d * 2 + bm * d * (4 + 4 + 2)) + 3 * bm * d * 4
    return pl.pallas_call(
        _xattn_kernel,
        grid=(bsz, s // bm),
        in_specs=[row(), pl.BlockSpec((None, mlen, 2 * d), lambda bi, i: (bi, 0, 0)),
                  full(d, d), row(), full(1, d), full(1, d)],
        out_specs=[row(), row()],
        out_shape=[jax.ShapeDtypeStruct((bsz, s, d), F32),
                   jax.ShapeDtypeStruct((bsz, s, d), BF16)],
        compiler_params=_params(("parallel", "parallel"), vmem + (4 << 20)),
        name="mem_xattn_ln",
    )(q, kv, w_o, h, g, b)


def _ffn_kernel(xb_ref, wg_ref, wu_ref, wd_ref, h_ref, g_ref, b_ref, o_ref, acc_ref):
    j = pl.program_id(1)

    @pl.when(j == 0)
    def _():
        acc_ref[...] = jnp.zeros_like(acc_ref)

    x = xb_ref[...]
    gate = _dot(x, wg_ref[...])
    up = _dot(x, wu_ref[...])
    act = gate / (1.0 + jnp.exp(-gate)) * up
    acc_ref[...] += _dot(act.astype(BF16), wd_ref[...])

    @pl.when(j == pl.num_programs(1) - 1)
    def _():
        o_ref[...] = _layer_norm(ALPHA * h_ref[...] + acc_ref[...], g_ref[...], b_ref[...])


def _ffn_ln(hb, w_gu, w_d, h, g, b, bm, bf):
    m, d = h.shape
    dff = w_d.shape[0]
    nf = dff // bf
    assert dff % bf == 0 and m % bm == 0
    row = lambda: pl.BlockSpec((bm, d), lambda i, j: (i, 0))
    full = lambda: pl.BlockSpec((1, d), lambda i, j: (0, 0))
    vmem = 2 * (bm * d * 2 + 3 * d * bf * 2 + 2 * bm * d * 4) + bm * d * 4 + 4 * bm * bf * 4
    return pl.pallas_call(
        _ffn_kernel,
        grid=(m // bm, nf),
        in_specs=[row(),
                  pl.BlockSpec((d, bf), lambda i, j: (0, j)),
                  pl.BlockSpec((d, bf), lambda i, j: (0, j + nf)),
                  pl.BlockSpec((bf, d), lambda i, j: (j, 0)),
                  row(), full(), full()],
        out_specs=row(),
        out_shape=jax.ShapeDtypeStruct((m, d), F32),
        scratch_shapes=[pltpu.VMEM((bm, d), F32)],
        compiler_params=_params(("parallel", "arbitrary"), vmem + (4 << 20)),
        name="swiglu_ffn_ln",
    )(hb, w_gu, w_gu, w_d, h, g, b)


def kernel(x, mem, w_in, sinks, g_swa, g_sb, w_o, ln1_g, ln1_b, w_q_mem, w_kv_mem,
           w_o_mem, ln2_g, ln2_b, w_gate_up, w_down, ln3_g, ln3_b):
    bsz, s, d = x.shape
    t = bsz * s
    mlen = mem.shape[1]
    h = x.reshape(t, d)
    for l in range(DEPTH):
        row = lambda a: a[l].reshape(1, -1)
        hb = h.astype(BF16)
        qkv = _matmul(hb, w_in[l].astype(BF16), 1024, 1152, BF16, "qkv_proj")
        qkv = qkv.reshape(bsz, s, D_IN)
        o_a = _swa_attention(qkv, sinks[l], row(g_swa))
        o_b = _sb_attention(qkv, row(g_sb))
        h, hb = _proj_ln(o_a.reshape(t, D_SWA), o_b.reshape(t, D_SB), w_o[l].astype(BF16),
                         h, row(ln1_g), row(ln1_b), 512)
        q = _matmul(hb, w_q_mem[l].astype(BF16), 1024, 1024, BF16, "mem_q_proj")
        kv = _matmul(mem.reshape(bsz * mlen, d).astype(BF16), w_kv_mem[l].astype(BF16),
                     1024, 1024, BF16, "mem_kv_proj")
        h3, hb3 = _xattn_ln(q.reshape(bsz, s, d), kv.reshape(bsz, mlen, 2 * d),
                            w_o_mem[l].astype(BF16), h.reshape(bsz, s, d),
                            row(ln2_g), row(ln2_b), 256)
        h, hb = h3.reshape(t, d), hb3.reshape(t, d)
        h = _ffn_ln(hb, w_gate_up[l].astype(BF16), w_down[l].astype(BF16), h,
                    row(ln3_g), row(ln3_b), 512, 512)
    return h.reshape(bsz, s, d)
```

```python
import functools
import math

import jax
import jax.numpy as jnp
from jax import lax
from jax.experimental import pallas as pl
from jax.experimental.pallas import tpu as pltpu

F32 = jnp.float32
BF16 = jnp.bfloat16

D_MODEL = 2048
HEAD_DIM = 64
N_HEADS_SWA = 16
N_KV_SWA = 4
N_HEADS_SB = 16
WINDOW = 128
BLOCK = 128
N_HEADS_MEM = 4
HEAD_DIM_MEM = D_MODEL // N_HEADS_MEM
D_SWA = N_HEADS_SWA * HEAD_DIM
D_KV_SWA = N_KV_SWA * HEAD_DIM
D_SB = N_HEADS_SB * HEAD_DIM
D_IN = D_SWA + 2 * D_KV_SWA + 3 * D_SB
DEPTH = 1
ALPHA = (2.0 * DEPTH) ** 0.25
LN_EPS = 1e-5
RMS_EPS = 1e-6
QK_SCALE = 1.0 / math.sqrt(HEAD_DIM)

LANES = 128
V7X_VMEM_LIMIT = 56 * 1024 * 1024

_KA_BLK = D_SWA // LANES
_VA_BLK = (D_SWA + D_KV_SWA) // LANES
_QB_BLK = (D_SWA + 2 * D_KV_SWA) // LANES
_KB_BLK = _QB_BLK + D_SB // LANES
_VB_BLK = _KB_BLK + D_SB // LANES


def _params(sem, vmem_bytes, flags=None):
    return pltpu.CompilerParams(
        dimension_semantics=sem, vmem_limit_bytes=min(int(vmem_bytes), V7X_VMEM_LIMIT),
        flags=flags)


def _dot(a, b):
    return jnp.dot(a, b, preferred_element_type=F32)


def _dot_nt(a, b):
    return lax.dot_general(a, b, (((1,), (1,)), ((), ())), preferred_element_type=F32)


def _layer_norm(y, g, b):
    mu = jnp.mean(y, axis=-1, keepdims=True)
    d = y - mu
    var = jnp.mean(d * d, axis=-1, keepdims=True)
    return d * lax.rsqrt(var + LN_EPS) * g + b


def _mm_kernel(x_ref, w_ref, o_ref, *xb_ref):
    if xb_ref:
        @pl.when(pl.program_id(1) == 0)
        def _():
            xb_ref[0][...] = x_ref[...].astype(BF16)
        x = xb_ref[0][...]
    else:
        x = x_ref[...]
    o_ref[...] = _dot(x, w_ref[...]).astype(o_ref.dtype)


def _matmul(x, w, bm, bn, out_dtype, name):
    m, k = x.shape
    n = w.shape[1]
    assert m % bm == 0 and n % bn == 0
    osz = jnp.dtype(out_dtype).itemsize
    xsz = jnp.dtype(x.dtype).itemsize
    cast = x.dtype != BF16
    vmem = 2 * (bm * k * xsz + k * bn * 2 + bm * bn * osz) + bm * bn * 4 + cast * bm * k * 2
    return pl.pallas_call(
        _mm_kernel,
        grid=(m // bm, n // bn),
        in_specs=[pl.BlockSpec((bm, k), lambda i, j: (i, 0)),
                  pl.BlockSpec((k, bn), lambda i, j: (0, j))],
        out_specs=pl.BlockSpec((bm, bn), lambda i, j: (i, j)),
        out_shape=jax.ShapeDtypeStruct((m, n), out_dtype),
        scratch_shapes=[pltpu.VMEM((bm, k), BF16)] if cast else [],
        compiler_params=_params(("parallel", "arbitrary"), vmem + (8 << 20)),
        name=name,
    )(x, w)


def _alibi_slope(h):
    return 2.0 ** (-8.0 * (h + 1) / N_HEADS_SWA)


_SWA_CHUNK = 32


def _swa_kernel(sink_ref, q_ref, kp_ref, kc_ref, vp_ref, vc_ref, g_ref, o_ref,
                qs_ref, k_s, v_s, s_ref, p_ref, den_ref, pv_ref):
    kvp = pl.program_id(1)
    n = pl.program_id(2)
    ch = _SWA_CHUNK
    k_s[:BLOCK] = kp_ref[...]
    k_s[BLOCK:] = kc_ref[...]
    v_s[:BLOCK] = vp_ref[...]
    v_s[BLOCK:] = vc_ref[...]
    lane = lax.broadcasted_iota(jnp.int32, (BLOCK, LANES), 1)
    halves = [lane < HEAD_DIM, lane >= HEAD_DIM]

    def head_rows(t, u):
        r0 = ((t % 2) * 2 + u) * BLOCK
        return slice(r0, r0 + BLOCK)

    for t in range(4):
        c = t // 2
        q_t = q_ref[:, t * LANES:(t + 1) * LANES]
        q_sw = pltpu.roll(q_t.astype(F32), HEAD_DIM, 1).astype(BF16)
        for u in range(2):
            src = q_t if u == c else q_sw
            qs_ref[c, head_rows(t, u), :] = jnp.where(halves[c], src, jnp.zeros_like(src)) * QK_SCALE

    for c in range(2):
        s_ref[c] = _dot_nt(qs_ref[c], k_s[...])

    qi = lax.broadcasted_iota(jnp.int32, (ch, 2 * BLOCK), 0)
    kj = lax.broadcasted_iota(jnp.int32, (ch, 2 * BLOCK), 1)
    pens = []
    for r0 in range(0, BLOCK, ch):
        dist = qi + (r0 + BLOCK) - kj
        valid = (dist >= 0) & (dist < WINDOW) & ((kj >= BLOCK) | (n > 0))
        pens.append(jnp.where(valid, dist.astype(F32), jnp.inf))

    for c in range(2):
        for t in (2 * c, 2 * c + 1):
            for u in range(2):
                slope = jnp.where(kvp == 0, _alibi_slope(t * 2 + u), _alibi_slope(8 + t * 2 + u))
                sink = sink_ref[kvp * 8 + t * 2 + u]
                for i, pen in enumerate(pens):
                    r0 = head_rows(t, u).start + i * ch
                    rows = slice(r0, r0 + ch)
                    s = s_ref[c, rows, :] - slope * pen
                    m = jnp.maximum(jnp.max(s, axis=-1, keepdims=True), sink)
                    p = jnp.exp(s - m)
                    den = jnp.sum(p, axis=-1, keepdims=True) + jnp.exp(sink - m)
                    p_ref[c, rows, :] = p.astype(BF16)
                    den_ref[c, rows, :] = jnp.broadcast_to(den, (ch, LANES))
        pv_ref[c] = _dot(p_ref[c], v_s[...])

    for t in range(4):
        c = t // 2
        ys = []
        for u in range(2):
            rows = head_rows(t, u)
            o = pv_ref[c, rows, :] / den_ref[c, rows, :]
            ms = jnp.sum(jnp.where(halves[c], o * o, 0.0), axis=-1, keepdims=True) / HEAD_DIM
            y = o * lax.rsqrt(ms + RMS_EPS)
            if u != c:
                y = pltpu.roll(y, HEAD_DIM, 1)
            ys.append(y)
        y2 = jnp.where(halves[0], ys[0], ys[1])
        o_ref[:, t * LANES:(t + 1) * LANES] = (
            y2 * g_ref[:, t * LANES:(t + 1) * LANES]).astype(o_ref.dtype)


def _swa_attention(qkv, sinks, g_swa):
    b, s, _ = qkv.shape
    nb = s // BLOCK
    qw = 4 * LANES
    kv_spec = lambda blk, prev: pl.BlockSpec(
        (None, BLOCK, LANES),
        (lambda bi, kp, n: (bi, jnp.maximum(n - 1, 0), blk + kp)) if prev
        else (lambda bi, kp, n: (bi, n, blk + kp)))
    return pl.pallas_call(
        _swa_kernel,
        grid=(b, 2, nb),
        in_specs=[pl.BlockSpec(memory_space=pltpu.SMEM),
                  pl.BlockSpec((None, BLOCK, qw), lambda bi, kp, n: (bi, n, kp)),
                  kv_spec(_KA_BLK, True), kv_spec(_KA_BLK, False),
                  kv_spec(_VA_BLK, True), kv_spec(_VA_BLK, False),
                  pl.BlockSpec((1, qw), lambda bi, kp, n: (0, kp))],
        out_specs=pl.BlockSpec((None, BLOCK, qw), lambda bi, kp, n: (bi, n, kp)),
        out_shape=jax.ShapeDtypeStruct((b, s, D_SWA), BF16),
        scratch_shapes=[pltpu.VMEM((2, 4 * BLOCK, LANES), BF16),
                        pltpu.VMEM((2 * BLOCK, LANES), BF16),
                        pltpu.VMEM((2 * BLOCK, LANES), BF16),
                        pltpu.VMEM((2, 4 * BLOCK, 2 * BLOCK), F32),
                        pltpu.VMEM((2, 4 * BLOCK, 2 * BLOCK), BF16),
                        pltpu.VMEM((2, 4 * BLOCK, LANES), F32),
                        pltpu.VMEM((2, 4 * BLOCK, LANES), F32)],
        compiler_params=_params(("parallel", "parallel", "parallel"), 16 << 20),
        name="swa_attention",
    )(sinks, qkv, qkv, qkv, qkv, qkv, g_swa)


_SB_TK = 256
_SB_TQ = 512
_SB_CHUNK = 32
_SB_MASKED = -1e30
_LOG2E = 1.4426950408889634


def _sb_kernel(q_ref, k_ref, v_ref, nu_ref, g_ref, o_ref,
               qm_ref, z_ref, hl_ref, bt_ref, a_ref, acc_ref, car_ref):
    qb = pl.program_id(2)
    tq, tk, ch = _SB_TQ, _SB_TK, _SB_CHUNK
    ntile = tq // tk
    last = qb * ntile + ntile - 1
    lane = lax.broadcasted_iota(jnp.int32, (tq, LANES), 1)
    q = q_ref[...] * QK_SCALE
    qm_ref[0] = jnp.where(lane < HEAD_DIM, q, jnp.zeros_like(q))
    qm_ref[1] = jnp.where(lane >= HEAD_DIM, q, jnp.zeros_like(q))
    acc_ref[...] = jnp.zeros_like(acc_ref)
    car_ref[...] = jnp.zeros_like(car_ref)
    cmr = (lax.broadcasted_iota(jnp.int32, (ch, tk), 1)
           - lax.broadcasted_iota(jnp.int32, (ch, tk), 0))

    def key_tile(ref, j):
        return ref[pl.ds(pl.multiple_of(j * tk, tk), tk), :]

    def z_stage(slot, j, r_lo):
        k = key_tile(k_ref, j)
        for h in range(2):
            z_ref[slot, h, r_lo:, :] = _dot_nt(qm_ref[h, r_lo:, :], k)

    def p_stage(slot, h, off, r_lo):
        for r0 in range(r_lo, tq, ch):
            rows = slice(r0, r0 + ch)
            t = z_ref[slot, h, rows, :] * _LOG2E
            p = jnp.maximum(t, 0.0) + jnp.log2(1.0 + jnp.exp2(-jnp.abs(t)))
            car = car_ref[h, rows, :]
            tc = t + jnp.concatenate([car, car], axis=1)
            if off is not None:
                causal = cmr < off + r0
                p = jnp.where(causal, p, 0.0)
                tc = jnp.where(causal, tc, _SB_MASKED)
            hi = p.astype(BF16)
            lo = (p - hi.astype(F32)).astype(BF16)
            hl_ref[h, rows, :tk] = hi
            hl_ref[h, rows, tk:] = lo
            z_ref[slot, h, rows, :] = tc
            car_ref[h, rows, :] = car - jnp.sum(p, axis=-1, keepdims=True)

    def sum_stage(h, r_lo):
        bt_ref[h, r_lo:, :] = _dot(hl_ref[h, r_lo:, :], nu_ref[...])

    def w_stage(slot, h, r_lo):
        for r0 in range(r_lo, tq, ch):
            rows = slice(r0, r0 + ch)
            a_ref[h, rows, :] = jnp.exp2(z_ref[slot, h, rows, :] + bt_ref[h, rows, :]).astype(BF16)

    def pv_stage(j, r_lo):
        v = key_tile(v_ref, j)
        for h in range(2):
            acc_ref[h, r_lo:, :] += _dot(a_ref[h, r_lo:, :], v)

    def body(j, slot, off, r_lo=0, r_prev=0, r_next=0):
        if r_prev is not None:
            pv_stage(j + 1, r_prev)
        z_stage(1 - slot, jnp.maximum(j - 1, 0), r_next)
        p_stage(slot, 0, off, r_lo)
        sum_stage(0, r_lo)
        p_stage(slot, 1, off, r_lo)
        sum_stage(1, r_lo)
        w_stage(slot, 0, r_lo)
        w_stage(slot, 1, r_lo)

    assert ntile % 2 == 0
    first_row = [(ntile - 1 - i) * tk for i in range(ntile)] + [0]
    z_stage(0, last, first_row[0])
    for i in range(ntile):
        body(last - i, i % 2, (i + 1 - ntile) * tk, first_row[i],
             first_row[i - 1] if i else None, first_row[i + 1])

    @pl.loop(0, qb * (ntile // 2))
    def _(m):
        j = qb * ntile - 1 - 2 * m
        body(j, 0, None)
        body(j - 1, 1, None)

    pv_stage(0, 0)

    ys = []
    for h, half in enumerate([lane < HEAD_DIM, lane >= HEAD_DIM]):
        o = acc_ref[h]
        ms = jnp.sum(jnp.where(half, o * o, 0.0), axis=-1, keepdims=True) / HEAD_DIM
        ys.append(o * lax.rsqrt(ms + RMS_EPS))
    o_ref[...] = (jnp.where(lane < HEAD_DIM, ys[0], ys[1]) * g_ref[...]).astype(o_ref.dtype)


def _sb_attention(qkv, g_sb):
    b, s, _ = qkv.shape
    npair = D_SB // LANES
    tq, tk = _SB_TQ, _SB_TK
    r = lax.broadcasted_iota(jnp.int32, (2 * tk, tk), 0) % tk
    c = lax.broadcasted_iota(jnp.int32, (2 * tk, tk), 1)
    nu = -(r >= c).astype(BF16)
    return pl.pallas_call(
        _sb_kernel,
        grid=(b, npair, s // tq),
        in_specs=[pl.BlockSpec((None, tq, LANES), lambda bi, p, i: (bi, i, _QB_BLK + p)),
                  pl.BlockSpec((None, s, LANES), lambda bi, p, i: (bi, 0, _KB_BLK + p)),
                  pl.BlockSpec((None, s, LANES), lambda bi, p, i: (bi, 0, _VB_BLK + p)),
                  pl.BlockSpec((2 * tk, tk), lambda bi, p, i: (0, 0)),
                  pl.BlockSpec((1, LANES), lambda bi, p, i: (0, p))],
        out_specs=pl.BlockSpec((None, tq, LANES), lambda bi, p, i: (bi, i, p)),
        out_shape=jax.ShapeDtypeStruct((b, s, D_SB), BF16),
        scratch_shapes=[pltpu.VMEM((2, tq, LANES), BF16),
                        pltpu.VMEM((2, 2, tq, tk), F32),
                        pltpu.VMEM((2, tq, 2 * tk), BF16),
                        pltpu.VMEM((2, tq, tk), F32),
                        pltpu.VMEM((2, tq, tk), BF16),
                        pltpu.VMEM((2, tq, LANES), F32),
                        pltpu.VMEM((2, tq, LANES), F32)],
        compiler_params=_params(("parallel", "parallel", "parallel"), 32 << 20),
        name="sb_attention",
    )(qkv, qkv, qkv, nu, g_sb)


def _proj_ln_kernel(oa_ref, ob_ref, w_ref, x_ref, g_ref, b_ref, h_ref, hb_ref):
    mix = _dot(oa_ref[...], w_ref[:D_SWA, :]) + _dot(ob_ref[...], w_ref[D_SWA:, :])
    h = _layer_norm(ALPHA * x_ref[...] + mix, g_ref[...], b_ref[...])
    h_ref[...] = h
    hb_ref[...] = h.astype(BF16)


def _proj_ln(o_a, o_b, w_o, x, g, b, bm):
    m, d = x.shape
    row = lambda w: pl.BlockSpec((bm, w), lambda i: (i, 0))
    full = lambda r, w: pl.BlockSpec((r, w), lambda i: (0, 0))
    vmem = 2 * (2 * bm * D_SWA * 2 + d * d * 2 + bm * d * (4 + 4 + 2)) + 2 * bm * d * 4
    return pl.pallas_call(
        _proj_ln_kernel,
        grid=(m // bm,),
        in_specs=[row(D_SWA), row(D_SB), full(d, d), row(d), full(1, d), full(1, d)],
        out_specs=[row(d), row(d)],
        out_shape=[jax.ShapeDtypeStruct((m, d), F32), jax.ShapeDtypeStruct((m, d), BF16)],
        compiler_params=_params(("parallel",), vmem + (4 << 20)),
        name="mix_proj_ln",
    )(o_a, o_b, w_o, x, g, b)


def _xattn_kernel(q_ref, kv_ref, w_ref, h_ref, g_ref, b_ref, o_ref, ob_ref):
    scale = 1.0 / math.sqrt(HEAD_DIM_MEM)
    outs = []
    for hd in range(N_HEADS_MEM):
        lo, hi = hd * HEAD_DIM_MEM, (hd + 1) * HEAD_DIM_MEM
        s = _dot_nt(q_ref[:, lo:hi], kv_ref[:, lo:hi]) * scale
        p = jnp.exp(s - jnp.max(s, axis=-1, keepdims=True))
        l = jnp.sum(p, axis=-1, keepdims=True)
        o = _dot(p.astype(BF16), kv_ref[:, D_MODEL + lo:D_MODEL + hi]) / l
        outs.append(o.astype(BF16))
    c = _dot(jnp.concatenate(outs, axis=-1), w_ref[...])
    h = _layer_norm(ALPHA * h_ref[...] + c, g_ref[...], b_ref[...])
    o_ref[...] = h
    ob_ref[...] = h.astype(BF16)


def _xattn_ln(q, kv, w_o, h, g, b, bm):
    bsz, s, d = q.shape
    mlen = kv.shape[1]
    row = lambda: pl.BlockSpec((None, bm, d), lambda bi, i: (bi, i, 0))
    full = lambda r, w: pl.BlockSpec((r, w), lambda bi, i: (0, 0))
    vmem = 2 * (bm * d * 2 + mlen * 2 * d * 2 + d * d * 2 + bm * d * (4 + 4 + 2)) + 3 * bm * d * 4
    return pl.pallas_call(
        _xattn_kernel,
        grid=(bsz, s // bm),
        in_specs=[row(), pl.BlockSpec((None, mlen, 2 * d), lambda bi, i: (bi, 0, 0)),
                  full(d, d), row(), full(1, d), full(1, d)],
        out_specs=[row(), row()],
        out_shape=[jax.ShapeDtypeStruct((bsz, s, d), F32),
                   jax.ShapeDtypeStruct((bsz, s, d), BF16)],
        compiler_params=_params(("parallel", "parallel"), vmem + (4 << 20)),
        name="mem_xattn_ln",
    )(q, kv, w_o, h, g, b)


def _ffn_kernel(xb_ref, wg_ref, wu_ref, wd_ref, h_ref, g_ref, b_ref, o_ref, acc_ref, act_ref):
    j = pl.program_id(1)
    nf = pl.num_programs(1) - 1

    def activations():
        x = xb_ref[...]
        gate = _dot(x, wg_ref[...])
        up = _dot(x, wu_ref[...])
        act_ref[...] = (gate / (1.0 + jnp.exp(-gate)) * up).astype(BF16)

    def down():
        return _dot(act_ref[...], wd_ref[...])

    @pl.when(j == 0)
    def _():
        activations()

    @pl.when(j == 1)
    def _():
        acc_ref[...] = down()
        activations()

    @pl.when((j > 1) & (j < nf))
    def _():
        acc_ref[...] += down()
        activations()

    @pl.when(j == nf)
    def _():
        f = acc_ref[...] + down()
        o_ref[...] = _layer_norm(ALPHA * h_ref[...] + f, g_ref[...], b_ref[...])


def _ffn_ln(hb, w_gu, w_d, h, g, b, bm, bf):
    m, d = h.shape
    dff = w_d.shape[0]
    nf = dff // bf
    assert dff % bf == 0 and m % bm == 0 and nf >= 2
    row = lambda: pl.BlockSpec((bm, d), lambda i, j: (i, 0))
    full = lambda: pl.BlockSpec((1, d), lambda i, j: (0, 0))
    vmem = (2 * (bm * d * 2 + 3 * d * bf * 2 + 2 * bm * d * 4) + bm * d * 4 + 4 * bm * bf * 4
            + bm * bf * 2)
    return pl.pallas_call(
        _ffn_kernel,
        grid=(m // bm, nf + 1),
        in_specs=[row(),
                  pl.BlockSpec((d, bf), lambda i, j: (0, jnp.minimum(j, nf - 1))),
                  pl.BlockSpec((d, bf), lambda i, j: (0, jnp.minimum(j, nf - 1) + nf)),
                  pl.BlockSpec((bf, d), lambda i, j: (jnp.maximum(j - 1, 0), 0)),
                  row(), full(), full()],
        out_specs=row(),
        out_shape=jax.ShapeDtypeStruct((m, d), F32),
        scratch_shapes=[pltpu.VMEM((bm, d), F32), pltpu.VMEM((bm, bf), BF16)],
        compiler_params=_params(("parallel", "arbitrary"), vmem + (4 << 20)),
        name="swiglu_ffn_ln",
    )(hb, w_gu, w_gu, w_d, h, g, b)


def kernel(x, mem, w_in, sinks, g_swa, g_sb, w_o, ln1_g, ln1_b, w_q_mem, w_kv_mem,
           w_o_mem, ln2_g, ln2_b, w_gate_up, w_down, ln3_g, ln3_b):
    bsz, s, d = x.shape
    t = bsz * s
    mlen = mem.shape[1]
    h = x.reshape(t, d)
    for l in range(DEPTH):
        row = lambda a: a[l].reshape(1, -1)
        qkv = _matmul(h, w_in[l].astype(BF16), 1024, 1152, BF16, "qkv_proj")
        qkv = qkv.reshape(bsz, s, D_IN)
        o_a = _swa_attention(qkv, sinks[l], row(g_swa))
        o_b = _sb_attention(qkv, row(g_sb))
        h, hb = _proj_ln(o_a.reshape(t, D_SWA), o_b.reshape(t, D_SB), w_o[l].astype(BF16),
                         h, row(ln1_g), row(ln1_b), 512)
        q = _matmul(hb, w_q_mem[l].astype(BF16), 1024, 1024, BF16, "mem_q_proj")
        kv = _matmul(mem.reshape(bsz * mlen, d), w_kv_mem[l].astype(BF16),
                     1024, 1024, BF16, "mem_kv_proj")
        h3, hb3 = _xattn_ln(q.reshape(bsz, s, d), kv.reshape(bsz, mlen, 2 * d),
                            w_o_mem[l].astype(BF16), h.reshape(bsz, s, d),
                            row(ln2_g), row(ln2_b), 256)
        h, hb = h3.reshape(t, d), hb3.reshape(t, d)
        h = _ffn_ln(hb, w_gate_up[l].astype(BF16), w_down[l].astype(BF16), h,
                    row(ln3_g), row(ln3_b), 512, 512)
    return h.reshape(bsz, s, d)
```

```python
import functools
import math

import jax
import jax.numpy as jnp
from jax import lax
from jax.experimental import pallas as pl
from jax.experimental.pallas import tpu as pltpu

F32 = jnp.float32
BF16 = jnp.bfloat16

D_MODEL = 2048
HEAD_DIM = 64
N_HEADS_SWA = 16
N_KV_SWA = 4
N_HEADS_SB = 16
WINDOW = 128
BLOCK = 128
N_HEADS_MEM = 4
HEAD_DIM_MEM = D_MODEL // N_HEADS_MEM
D_SWA = N_HEADS_SWA * HEAD_DIM
D_KV_SWA = N_KV_SWA * HEAD_DIM
D_SB = N_HEADS_SB * HEAD_DIM
D_IN = D_SWA + 2 * D_KV_SWA + 3 * D_SB
DEPTH = 1
ALPHA = (2.0 * DEPTH) ** 0.25
LN_EPS = 1e-5
RMS_EPS = 1e-6
QK_SCALE = 1.0 / math.sqrt(HEAD_DIM)

LANES = 128
V7X_VMEM_LIMIT = 56 * 1024 * 1024

_KA_BLK = D_SWA // LANES
_VA_BLK = (D_SWA + D_KV_SWA) // LANES
_QB_BLK = (D_SWA + 2 * D_KV_SWA) // LANES
_KB_BLK = _QB_BLK + D_SB // LANES
_VB_BLK = _KB_BLK + D_SB // LANES


def _params(sem, vmem_bytes, flags=None):
    return pltpu.CompilerParams(
        dimension_semantics=sem, vmem_limit_bytes=min(int(vmem_bytes), V7X_VMEM_LIMIT),
        flags=flags)


def _dot(a, b):
    return jnp.dot(a, b, preferred_element_type=F32)


def _dot_nt(a, b):
    return lax.dot_general(a, b, (((1,), (1,)), ((), ())), preferred_element_type=F32)


def _layer_norm(y, g, b):
    mu = jnp.mean(y, axis=-1, keepdims=True)
    d = y - mu
    var = jnp.mean(d * d, axis=-1, keepdims=True)
    return d * lax.rsqrt(var + LN_EPS) * g + b


def _mm_kernel(x_ref, w_ref, o_ref, *xb_ref):
    if xb_ref:
        @pl.when(pl.program_id(1) == 0)
        def _():
            xb_ref[0][...] = x_ref[...].astype(BF16)
        x = xb_ref[0][...]
    else:
        x = x_ref[...]
    o_ref[...] = _dot(x, w_ref[...]).astype(o_ref.dtype)


def _matmul(x, w, bm, bn, out_dtype, name):
    m, k = x.shape
    n = w.shape[1]
    assert m % bm == 0 and n % bn == 0
    osz = jnp.dtype(out_dtype).itemsize
    xsz = jnp.dtype(x.dtype).itemsize
    cast = x.dtype != BF16
    vmem = 2 * (bm * k * xsz + k * bn * 2 + bm * bn * osz) + bm * bn * 4 + cast * bm * k * 2
    return pl.pallas_call(
        _mm_kernel,
        grid=(m // bm, n // bn),
        in_specs=[pl.BlockSpec((bm, k), lambda i, j: (i, 0)),
                  pl.BlockSpec((k, bn), lambda i, j: (0, j))],
        out_specs=pl.BlockSpec((bm, bn), lambda i, j: (i, j)),
        out_shape=jax.ShapeDtypeStruct((m, n), out_dtype),
        scratch_shapes=[pltpu.VMEM((bm, k), BF16)] if cast else [],
        compiler_params=_params(("parallel", "arbitrary"), vmem + (8 << 20)),
        name=name,
    )(x, w)


def _alibi_slope(h):
    return 2.0 ** (-8.0 * (h + 1) / N_HEADS_SWA)


_SWA_CHUNK = 32


def _swa_kernel(sink_ref, q_ref, kp_ref, kc_ref, vp_ref, vc_ref, g_ref, o_ref,
                qs_ref, k_s, v_s, s_ref, p_ref, den_ref, pv_ref):
    kvp = pl.program_id(1)
    n = pl.program_id(2)
    ch = _SWA_CHUNK
    k_s[:BLOCK] = kp_ref[...]
    k_s[BLOCK:] = kc_ref[...]
    v_s[:BLOCK] = vp_ref[...]
    v_s[BLOCK:] = vc_ref[...]
    lane = lax.broadcasted_iota(jnp.int32, (BLOCK, LANES), 1)
    halves = [lane < HEAD_DIM, lane >= HEAD_DIM]

    def head_rows(t, u):
        r0 = ((t % 2) * 2 + u) * BLOCK
        return slice(r0, r0 + BLOCK)

    for t in range(4):
        c = t // 2
        q_t = q_ref[:, t * LANES:(t + 1) * LANES]
        q_sw = pltpu.roll(q_t.astype(F32), HEAD_DIM, 1).astype(BF16)
        for u in range(2):
            src = q_t if u == c else q_sw
            qs_ref[c, head_rows(t, u), :] = jnp.where(halves[c], src, jnp.zeros_like(src)) * QK_SCALE

    for c in range(2):
        s_ref[c] = _dot_nt(qs_ref[c], k_s[...])

    qi = lax.broadcasted_iota(jnp.int32, (ch, 2 * BLOCK), 0)
    kj = lax.broadcasted_iota(jnp.int32, (ch, 2 * BLOCK), 1)
    pens = []
    for r0 in range(0, BLOCK, ch):
        dist = qi + (r0 + BLOCK) - kj
        valid = (dist >= 0) & (dist < WINDOW) & ((kj >= BLOCK) | (n > 0))
        pens.append(jnp.where(valid, dist.astype(F32), jnp.inf))

    for c in range(2):
        for t in (2 * c, 2 * c + 1):
            for u in range(2):
                slope = jnp.where(kvp == 0, _alibi_slope(t * 2 + u), _alibi_slope(8 + t * 2 + u))
                sink = sink_ref[kvp * 8 + t * 2 + u]
                for i, pen in enumerate(pens):
                    r0 = head_rows(t, u).start + i * ch
                    rows = slice(r0, r0 + ch)
                    s = s_ref[c, rows, :] - slope * pen
                    m = jnp.maximum(jnp.max(s, axis=-1, keepdims=True), sink)
                    p = jnp.exp(s - m)
                    den = jnp.sum(p, axis=-1, keepdims=True) + jnp.exp(sink - m)
                    p_ref[c, rows, :] = p.astype(BF16)
                    den_ref[c, rows, :] = jnp.broadcast_to(1.0 / den, (ch, LANES))
        pv_ref[c] = _dot(p_ref[c], v_s[...])

    for t in range(4):
        c = t // 2
        ys = []
        for u in range(2):
            rows = head_rows(t, u)
            o = pv_ref[c, rows, :] * den_ref[c, rows, :]
            ms = jnp.sum(jnp.where(halves[c], o * o, 0.0), axis=-1, keepdims=True) / HEAD_DIM
            y = o * lax.rsqrt(ms + RMS_EPS)
            if u != c:
                y = pltpu.roll(y, HEAD_DIM, 1)
            ys.append(y)
        y2 = jnp.where(halves[0], ys[0], ys[1])
        o_ref[:, t * LANES:(t + 1) * LANES] = (
            y2 * g_ref[:, t * LANES:(t + 1) * LANES]).astype(o_ref.dtype)


def _swa_attention(qkv, sinks, g_swa):
    b, s, _ = qkv.shape
    nb = s // BLOCK
    qw = 4 * LANES
    kv_spec = lambda blk, prev: pl.BlockSpec(
        (None, BLOCK, LANES),
        (lambda bi, kp, n: (bi, jnp.maximum(n - 1, 0), blk + kp)) if prev
        else (lambda bi, kp, n: (bi, n, blk + kp)))
    return pl.pallas_call(
        _swa_kernel,
        grid=(b, 2, nb),
        in_specs=[pl.BlockSpec(memory_space=pltpu.SMEM),
                  pl.BlockSpec((None, BLOCK, qw), lambda bi, kp, n: (bi, n, kp)),
                  kv_spec(_KA_BLK, True), kv_spec(_KA_BLK, False),
                  kv_spec(_VA_BLK, True), kv_spec(_VA_BLK, False),
                  pl.BlockSpec((1, qw), lambda bi, kp, n: (0, kp))],
        out_specs=pl.BlockSpec((None, BLOCK, qw), lambda bi, kp, n: (bi, n, kp)),
        out_shape=jax.ShapeDtypeStruct((b, s, D_SWA), BF16),
        scratch_shapes=[pltpu.VMEM((2, 4 * BLOCK, LANES), BF16),
                        pltpu.VMEM((2 * BLOCK, LANES), BF16),
                        pltpu.VMEM((2 * BLOCK, LANES), BF16),
                        pltpu.VMEM((2, 4 * BLOCK, 2 * BLOCK), F32),
                        pltpu.VMEM((2, 4 * BLOCK, 2 * BLOCK), BF16),
                        pltpu.VMEM((2, 4 * BLOCK, LANES), F32),
                        pltpu.VMEM((2, 4 * BLOCK, LANES), F32)],
        compiler_params=_params(("parallel", "parallel", "parallel"), 16 << 20),
        name="swa_attention",
    )(sinks, qkv, qkv, qkv, qkv, qkv, g_swa)


_SB_TK = 256
_SB_TQ = 512
_SB_CHUNK = 32
_SB_MASKED = -1e30
_LOG2E = 1.4426950408889634


def _sb_kernel(q_ref, k_ref, v_ref, nu_ref, g_ref, o_ref,
               qm_ref, z_ref, pb_ref, bt_ref, a_ref, acc_ref, car_ref):
    qb = pl.program_id(2)
    tq, tk, ch = _SB_TQ, _SB_TK, _SB_CHUNK
    ntile = tq // tk
    last = qb * ntile + ntile - 1
    lane = lax.broadcasted_iota(jnp.int32, (tq, LANES), 1)
    q = q_ref[...] * QK_SCALE
    qm_ref[0] = jnp.where(lane < HEAD_DIM, q, jnp.zeros_like(q))
    qm_ref[1] = jnp.where(lane >= HEAD_DIM, q, jnp.zeros_like(q))
    acc_ref[...] = jnp.zeros_like(acc_ref)
    car_ref[...] = jnp.zeros_like(car_ref)
    cmr = (lax.broadcasted_iota(jnp.int32, (ch, tk), 1)
           - lax.broadcasted_iota(jnp.int32, (ch, tk), 0))

    def key_tile(ref, j):
        return ref[pl.ds(pl.multiple_of(j * tk, tk), tk), :]

    def z_stage(slot, j, r_lo):
        k = key_tile(k_ref, j)
        for h in range(2):
            z_ref[slot, h, r_lo:, :] = _dot_nt(qm_ref[h, r_lo:, :], k)

    def p_stage(slot, h, off, r_lo):
        for r0 in range(r_lo, tq, ch):
            rows = slice(r0, r0 + ch)
            t = z_ref[slot, h, rows, :] * _LOG2E
            p = jnp.maximum(t, 0.0) + jnp.log2(1.0 + jnp.exp2(-jnp.abs(t)))
            car = car_ref[h, rows, :]
            tc = t + jnp.concatenate([car, car], axis=1)
            if off is not None:
                causal = cmr < off + r0
                p = jnp.where(causal, p, 0.0)
                tc = jnp.where(causal, tc, _SB_MASKED)
            pb_ref[h, rows, :] = p.astype(BF16)
            z_ref[slot, h, rows, :] = tc
            car_ref[h, rows, :] = car - jnp.sum(p, axis=-1, keepdims=True)

    def sum_stage(h, r_lo):
        bt_ref[h, r_lo:, :] = _dot(pb_ref[h, r_lo:, :], nu_ref[...])

    def w_stage(slot, h, r_lo):
        for r0 in range(r_lo, tq, ch):
            rows = slice(r0, r0 + ch)
            a_ref[h, rows, :] = jnp.exp2(z_ref[slot, h, rows, :] + bt_ref[h, rows, :]).astype(BF16)

    def pv_stage(j, r_lo):
        v = key_tile(v_ref, j)
        for h in range(2):
            acc_ref[h, r_lo:, :] += _dot(a_ref[h, r_lo:, :], v)

    def body(j, slot, off, r_lo=0, r_prev=0, r_next=0):
        if r_prev is not None:
            pv_stage(j + 1, r_prev)
        z_stage(1 - slot, jnp.maximum(j - 1, 0), r_next)
        p_stage(slot, 0, off, r_lo)
        sum_stage(0, r_lo)
        p_stage(slot, 1, off, r_lo)
        sum_stage(1, r_lo)
        w_stage(slot, 0, r_lo)
        w_stage(slot, 1, r_lo)

    assert ntile % 2 == 0
    first_row = [(ntile - 1 - i) * tk for i in range(ntile)] + [0]
    z_stage(0, last, first_row[0])
    for i in range(ntile):
        body(last - i, i % 2, (i + 1 - ntile) * tk, first_row[i],
             first_row[i - 1] if i else None, first_row[i + 1])

    @pl.loop(0, qb * (ntile // 2))
    def _(m):
        j = qb * ntile - 1 - 2 * m
        body(j, 0, None)
        body(j - 1, 1, None)

    pv_stage(0, 0)

    ys = []
    for h, half in enumerate([lane < HEAD_DIM, lane >= HEAD_DIM]):
        o = acc_ref[h]
        ms = jnp.sum(jnp.where(half, o * o, 0.0), axis=-1, keepdims=True) / HEAD_DIM
        ys.append(o * lax.rsqrt(ms + RMS_EPS))
    o_ref[...] = (jnp.where(lane < HEAD_DIM, ys[0], ys[1]) * g_ref[...]).astype(o_ref.dtype)


def _sb_attention(qkv, g_sb):
    b, s, _ = qkv.shape
    npair = D_SB // LANES
    tq, tk = _SB_TQ, _SB_TK
    r = lax.broadcasted_iota(jnp.int32, (tk, tk), 0)
    c = lax.broadcasted_iota(jnp.int32, (tk, tk), 1)
    nu = -(r >= c).astype(BF16)
    return pl.pallas_call(
        _sb_kernel,
        grid=(b, npair, s // tq),
        in_specs=[pl.BlockSpec((None, tq, LANES), lambda bi, p, i: (bi, i, _QB_BLK + p)),
                  pl.BlockSpec((None, s, LANES), lambda bi, p, i: (bi, 0, _KB_BLK + p)),
                  pl.BlockSpec((None, s, LANES), lambda bi, p, i: (bi, 0, _VB_BLK + p)),
                  pl.BlockSpec((tk, tk), lambda bi, p, i: (0, 0)),
                  pl.BlockSpec((1, LANES), lambda bi, p, i: (0, p))],
        out_specs=pl.BlockSpec((None, tq, LANES), lambda bi, p, i: (bi, i, p)),
        out_shape=jax.ShapeDtypeStruct((b, s, D_SB), BF16),
        scratch_shapes=[pltpu.VMEM((2, tq, LANES), BF16),
                        pltpu.VMEM((2, 2, tq, tk), F32),
                        pltpu.VMEM((2, tq, tk), BF16),
                        pltpu.VMEM((2, tq, tk), F32),
                        pltpu.VMEM((2, tq, tk), BF16),
                        pltpu.VMEM((2, tq, LANES), F32),
                        pltpu.VMEM((2, tq, LANES), F32)],
        compiler_params=_params(("parallel", "parallel", "parallel"), 32 << 20),
        name="sb_attention",
    )(qkv, qkv, qkv, nu, g_sb)


def _proj_ln_kernel(oa_ref, ob_ref, w_ref, x_ref, g_ref, b_ref, h_ref, hb_ref):
    mix = _dot(oa_ref[...], w_ref[:D_SWA, :]) + _dot(ob_ref[...], w_ref[D_SWA:, :])
    h = _layer_norm(ALPHA * x_ref[...] + mix, g_ref[...], b_ref[...])
    h_ref[...] = h
    hb_ref[...] = h.astype(BF16)


def _proj_ln(o_a, o_b, w_o, x, g, b, bm):
    m, d = x.shape
    row = lambda w: pl.BlockSpec((bm, w), lambda i: (i, 0))
    full = lambda r, w: pl.BlockSpec((r, w), lambda i: (0, 0))
    vmem = 2 * (2 * bm * D_SWA * 2 + d * d * 2 + bm * d * (4 + 4 + 2)) + 2 * bm * d * 4
    return pl.pallas_call(
        _proj_ln_kernel,
        grid=(m // bm,),
        in_specs=[row(D_SWA), row(D_SB), full(d, d), row(d), full(1, d), full(1, d)],
        out_specs=[row(d), row(d)],
        out_shape=[jax.ShapeDtypeStruct((m, d), F32), jax.ShapeDtypeStruct((m, d), BF16)],
        compiler_params=_params(("parallel",), vmem + (4 << 20)),
        name="mix_proj_ln",
    )(o_a, o_b, w_o, x, g, b)


def _xattn_kernel(q_ref, kv_ref, w_ref, h_ref, g_ref, b_ref, o_ref, ob_ref):
    scale = 1.0 / math.sqrt(HEAD_DIM_MEM)
    outs = []
    for hd in range(N_HEADS_MEM):
        lo, hi = hd * HEAD_DIM_MEM, (hd + 1) * HEAD_DIM_MEM
        s = _dot_nt(q_ref[:, lo:hi], kv_ref[:, lo:hi]) * scale
        p = jnp.exp(s - jnp.max(s, axis=-1, keepdims=True))
        l = jnp.sum(p, axis=-1, keepdims=True)
        o = _dot(p.astype(BF16), kv_ref[:, D_MODEL + lo:D_MODEL + hi]) / l
        outs.append(o.astype(BF16))
    c = _dot(jnp.concatenate(outs, axis=-1), w_ref[...])
    h = _layer_norm(ALPHA * h_ref[...] + c, g_ref[...], b_ref[...])
    o_ref[...] = h
    ob_ref[...] = h.astype(BF16)


def _xattn_ln(q, kv, w_o, h, g, b, bm):
    bsz, s, d = q.shape
    mlen = kv.shape[1]
    row = lambda: pl.BlockSpec((None, bm, d), lambda bi, i: (bi, i, 0))
    full = lambda r, w: pl.BlockSpec((r, w), lambda bi, i: (0, 0))
    vmem = 2 * (bm * d * 2 + mlen * 2 * d * 2 + d * d * 2 + bm * d * (4 + 4 + 2)) + 3 * bm * d * 4
    return pl.pallas_call(
        _xattn_kernel,
        grid=(bsz, s // bm),
        in_specs=[row(), pl.BlockSpec((None, mlen, 2 * d), lambda bi, i: (bi, 0, 0)),
                  full(d, d), row(), full(1, d), full(1, d)],
        out_specs=[row(), row()],
        out_shape=[jax.ShapeDtypeStruct((bsz, s, d), F32),
                   jax.ShapeDtypeStruct((bsz, s, d), BF16)],
        compiler_params=_params(("parallel", "parallel"), vmem + (4 << 20)),
        name="mem_xattn_ln",
    )(q, kv, w_o, h, g, b)


def _ffn_kernel(xb_ref, wg_ref, wu_ref, wd_ref, h_ref, g_ref, b_ref, o_ref, acc_ref):
    j = pl.program_id(1)

    @pl.when(j == 0)
    def _():
        acc_ref[...] = jnp.zeros_like(acc_ref)

    x = xb_ref[...]
    gate = _dot(x, wg_ref[...])
    up = _dot(x, wu_ref[...])
    act = gate / (1.0 + jnp.exp(-gate)) * up
    acc_ref[...] += _dot(act.astype(BF16), wd_ref[...])

    @pl.when(j == pl.num_programs(1) - 1)
    def _():
        o_ref[...] = _layer_norm(ALPHA * h_ref[...] + acc_ref[...], g_ref[...], b_ref[...])


def _ffn_ln(hb, w_gu, w_d, h, g, b, bm, bf):
    m, d = h.shape
    dff = w_d.shape[0]
    nf = dff // bf
    assert dff % bf == 0 and m % bm == 0
    row = lambda: pl.BlockSpec((bm, d), lambda i, j: (i, 0))
    full = lambda: pl.BlockSpec((1, d), lambda i, j: (0, 0))
    vmem = 2 * (bm * d * 2 + 3 * d * bf * 2 + 2 * bm * d * 4) + bm * d * 4 + 4 * bm * bf * 4
    return pl.pallas_call(
        _ffn_kernel,
        grid=(m // bm, nf),
        in_specs=[row(),
                  pl.BlockSpec((d, bf), lambda i, j: (0, j)),
                  pl.BlockSpec((d, bf), lambda i, j: (0, j + nf)),
                  pl.BlockSpec((bf, d), lambda i, j: (j, 0)),
                  row(), full(), full()],
        out_specs=row(),
        out_shape=jax.ShapeDtypeStruct((m, d), F32),
        scratch_shapes=[pltpu.VMEM((bm, d), F32)],
        compiler_params=_params(("parallel", "arbitrary"), vmem + (4 << 20)),
        name="swiglu_ffn_ln",
    )(hb, w_gu, w_gu, w_d, h, g, b)


def kernel(x, mem, w_in, sinks, g_swa, g_sb, w_o, ln1_g, ln1_b, w_q_mem, w_kv_mem,
           w_o_mem, ln2_g, ln2_b, w_gate_up, w_down, ln3_g, ln3_b):
    bsz, s, d = x.shape
    t = bsz * s
    mlen = mem.shape[1]
    h = x.reshape(t, d)
    for l in range(DEPTH):
        row = lambda a: a[l].reshape(1, -1)
        qkv = _matmul(h, w_in[l].astype(BF16), 1024, 1152, BF16, "qkv_proj")
        qkv = qkv.reshape(bsz, s, D_IN)
        o_a = _swa_attention(qkv, sinks[l], row(g_swa))
        o_b = _sb_attention(qkv, row(g_sb))
        h, hb = _proj_ln(o_a.reshape(t, D_SWA), o_b.reshape(t, D_SB), w_o[l].astype(BF16),
                         h, row(ln1_g), row(ln1_b), 512)
        q = _matmul(hb, w_q_mem[l].astype(BF16), 1024, 1024, BF16, "mem_q_proj")
        kv = _matmul(mem.reshape(bsz * mlen, d), w_kv_mem[l].astype(BF16),
                     1024, 1024, BF16, "mem_kv_proj")
        h3, hb3 = _xattn_ln(q.reshape(bsz, s, d), kv.reshape(bsz, mlen, 2 * d),
                            w_o_mem[l].astype(BF16), h.reshape(bsz, s, d),
                            row(ln2_g), row(ln2_b), 256)
        h, hb = h3.reshape(t, d), hb3.reshape(t, d)
        h = _ffn_ln(hb, w_gate_up[l].astype(BF16), w_down[l].astype(BF16), h,
                    row(ln3_g), row(ln3_b), 512, 512)
    return h.reshape(bsz, s, d)
```

```python
import functools
import math

import jax
import jax.numpy as jnp
from jax import lax
from jax.experimental import pallas as pl
from jax.experimental.pallas import tpu as pltpu

F32 = jnp.float32
BF16 = jnp.bfloat16

D_MODEL = 2048
HEAD_DIM = 64
N_HEADS_SWA = 16
N_KV_SWA = 4
N_HEADS_SB = 16
WINDOW = 128
BLOCK = 128
N_HEADS_MEM = 4
HEAD_DIM_MEM = D_MODEL // N_HEADS_MEM
D_SWA = N_HEADS_SWA * HEAD_DIM
D_KV_SWA = N_KV_SWA * HEAD_DIM
D_SB = N_HEADS_SB * HEAD_DIM
D_IN = D_SWA + 2 * D_KV_SWA + 3 * D_SB
DEPTH = 1
ALPHA = (2.0 * DEPTH) ** 0.25
LN_EPS = 1e-5
RMS_EPS = 1e-6
QK_SCALE = 1.0 / math.sqrt(HEAD_DIM)

LANES = 128
V7X_VMEM_LIMIT = 56 * 1024 * 1024

_KA_BLK = D_SWA // LANES
_VA_BLK = (D_SWA + D_KV_SWA) // LANES
_QB_BLK = (D_SWA + 2 * D_KV_SWA) // LANES
_KB_BLK = _QB_BLK + D_SB // LANES
_VB_BLK = _KB_BLK + D_SB // LANES


def _params(sem, vmem_bytes, flags=None):
    return pltpu.CompilerParams(
        dimension_semantics=sem, vmem_limit_bytes=min(int(vmem_bytes), V7X_VMEM_LIMIT),
        flags=flags)


def _dot(a, b):
    return jnp.dot(a, b, preferred_element_type=F32)


def _dot_nt(a, b):
    return lax.dot_general(a, b, (((1,), (1,)), ((), ())), preferred_element_type=F32)


def _layer_norm(y, g, b):
    mu = jnp.mean(y, axis=-1, keepdims=True)
    d = y - mu
    var = jnp.mean(d * d, axis=-1, keepdims=True)
    return d * lax.rsqrt(var + LN_EPS) * g + b


def _mm_kernel(x_ref, w_ref, o_ref, *xb_ref):
    if xb_ref:
        @pl.when(pl.program_id(1) == 0)
        def _():
            xb_ref[0][...] = x_ref[...].astype(BF16)
        x = xb_ref[0][...]
    else:
        x = x_ref[...]
    o_ref[...] = _dot(x, w_ref[...]).astype(o_ref.dtype)


def _matmul(x, w, bm, bn, out_dtype, name):
    m, k = x.shape
    n = w.shape[1]
    assert m % bm == 0 and n % bn == 0
    osz = jnp.dtype(out_dtype).itemsize
    xsz = jnp.dtype(x.dtype).itemsize
    cast = x.dtype != BF16
    vmem = 2 * (bm * k * xsz + k * bn * 2 + bm * bn * osz) + bm * bn * 4 + cast * bm * k * 2
    return pl.pallas_call(
        _mm_kernel,
        grid=(m // bm, n // bn),
        in_specs=[pl.BlockSpec((bm, k), lambda i, j: (i, 0)),
                  pl.BlockSpec((k, bn), lambda i, j: (0, j))],
        out_specs=pl.BlockSpec((bm, bn), lambda i, j: (i, j)),
        out_shape=jax.ShapeDtypeStruct((m, n), out_dtype),
        scratch_shapes=[pltpu.VMEM((bm, k), BF16)] if cast else [],
        compiler_params=_params(("parallel", "arbitrary"), vmem + (8 << 20)),
        name=name,
    )(x, w)


def _alibi_slope(h):
    return 2.0 ** (-8.0 * (h + 1) / N_HEADS_SWA)


_SWA_CHUNK = 32


def _swa_kernel(sink_ref, q_ref, kp_ref, kc_ref, vp_ref, vc_ref, g_ref, o_ref,
                qs_ref, k_s, v_s, s_ref, p_ref, den_ref, pv_ref):
    kvp = pl.program_id(1)
    n = pl.program_id(2)
    ch = _SWA_CHUNK
    k_s[:BLOCK] = kp_ref[...]
    k_s[BLOCK:] = kc_ref[...]
    v_s[:BLOCK] = vp_ref[...]
    v_s[BLOCK:] = vc_ref[...]
    lane = lax.broadcasted_iota(jnp.int32, (BLOCK, LANES), 1)
    halves = [lane < HEAD_DIM, lane >= HEAD_DIM]

    def head_rows(t, u):
        r0 = ((t % 2) * 2 + u) * BLOCK
        return slice(r0, r0 + BLOCK)

    for t in range(4):
        c = t // 2
        q_t = q_ref[:, t * LANES:(t + 1) * LANES]
        q_sw = pltpu.roll(q_t.astype(F32), HEAD_DIM, 1).astype(BF16)
        for u in range(2):
            src = q_t if u == c else q_sw
            qs_ref[c, head_rows(t, u), :] = jnp.where(halves[c], src, jnp.zeros_like(src)) * QK_SCALE

    for c in range(2):
        s_ref[c] = _dot_nt(qs_ref[c], k_s[...])

    qi = lax.broadcasted_iota(jnp.int32, (ch, 2 * BLOCK), 0)
    kj = lax.broadcasted_iota(jnp.int32, (ch, 2 * BLOCK), 1)
    pens = []
    for r0 in range(0, BLOCK, ch):
        dist = qi + (r0 + BLOCK) - kj
        valid = (dist >= 0) & (dist < WINDOW) & ((kj >= BLOCK) | (n > 0))
        pens.append(jnp.where(valid, dist.astype(F32), jnp.inf))

    for c in range(2):
        for t in (2 * c, 2 * c + 1):
            for u in range(2):
                slope = jnp.where(kvp == 0, _alibi_slope(t * 2 + u), _alibi_slope(8 + t * 2 + u))
                sink = sink_ref[kvp * 8 + t * 2 + u]
                for i, pen in enumerate(pens):
                    r0 = head_rows(t, u).start + i * ch
                    rows = slice(r0, r0 + ch)
                    s = s_ref[c, rows, :] - slope * pen
                    m = jnp.maximum(jnp.max(s, axis=-1, keepdims=True), sink)
                    p = jnp.exp(s - m)
                    den = jnp.sum(p, axis=-1, keepdims=True) + jnp.exp(sink - m)
                    p_ref[c, rows, :] = p.astype(BF16)
                    den_ref[c, rows, :] = jnp.broadcast_to(1.0 / den, (ch, LANES))
        pv_ref[c] = _dot(p_ref[c], v_s[...])

    for t in range(4):
        c = t // 2
        ys = []
        for u in range(2):
            rows = head_rows(t, u)
            o = pv_ref[c, rows, :] * den_ref[c, rows, :]
            ms = jnp.sum(jnp.where(halves[c], o * o, 0.0), axis=-1, keepdims=True) / HEAD_DIM
            y = o * lax.rsqrt(ms + RMS_EPS)
            if u != c:
                y = pltpu.roll(y, HEAD_DIM, 1)
            ys.append(y)
        y2 = jnp.where(halves[0], ys[0], ys[1])
        o_ref[:, t * LANES:(t + 1) * LANES] = (
            y2 * g_ref[:, t * LANES:(t + 1) * LANES]).astype(o_ref.dtype)


def _swa_attention(qkv, sinks, g_swa):
    b, s, _ = qkv.shape
    nb = s // BLOCK
    qw = 4 * LANES
    kv_spec = lambda blk, prev: pl.BlockSpec(
        (None, BLOCK, LANES),
        (lambda bi, kp, n: (bi, jnp.maximum(n - 1, 0), blk + kp)) if prev
        else (lambda bi, kp, n: (bi, n, blk + kp)))
    return pl.pallas_call(
        _swa_kernel,
        grid=(b, 2, nb),
        in_specs=[pl.BlockSpec(memory_space=pltpu.SMEM),
                  pl.BlockSpec((None, BLOCK, qw), lambda bi, kp, n: (bi, n, kp)),
                  kv_spec(_KA_BLK, True), kv_spec(_KA_BLK, False),
                  kv_spec(_VA_BLK, True), kv_spec(_VA_BLK, False),
                  pl.BlockSpec((1, qw), lambda bi, kp, n: (0, kp))],
        out_specs=pl.BlockSpec((None, BLOCK, qw), lambda bi, kp, n: (bi, n, kp)),
        out_shape=jax.ShapeDtypeStruct((b, s, D_SWA), BF16),
        scratch_shapes=[pltpu.VMEM((2, 4 * BLOCK, LANES), BF16),
                        pltpu.VMEM((2 * BLOCK, LANES), BF16),
                        pltpu.VMEM((2 * BLOCK, LANES), BF16),
                        pltpu.VMEM((2, 4 * BLOCK, 2 * BLOCK), F32),
                        pltpu.VMEM((2, 4 * BLOCK, 2 * BLOCK), BF16),
                        pltpu.VMEM((2, 4 * BLOCK, LANES), F32),
                        pltpu.VMEM((2, 4 * BLOCK, LANES), F32)],
        compiler_params=_params(("parallel", "parallel", "parallel"), 16 << 20),
        name="swa_attention",
    )(sinks, qkv, qkv, qkv, qkv, qkv, g_swa)


_SB_TK = 256
_SB_TQ = 512
_SB_CHUNK = 32
_SB_MASKED = -1e30
_LOG2E = 1.4426950408889634


def _sb_kernel(q_ref, k_ref, v_ref, nu_ref, g_ref, o_ref,
               qm_ref, z_ref, pb_ref, bt_ref, a_ref, acc_ref, car_ref):
    qb = pl.program_id(2)
    tq, tk, ch = _SB_TQ, _SB_TK, _SB_CHUNK
    ntile = tq // tk
    last = qb * ntile + ntile - 1
    lane = lax.broadcasted_iota(jnp.int32, (tq, LANES), 1)
    q = q_ref[...] * QK_SCALE
    qm_ref[0] = jnp.where(lane < HEAD_DIM, q, jnp.zeros_like(q))
    qm_ref[1] = jnp.where(lane >= HEAD_DIM, q, jnp.zeros_like(q))
    acc_ref[...] = jnp.zeros_like(acc_ref)
    car_ref[...] = jnp.zeros_like(car_ref)
    cmr = (lax.broadcasted_iota(jnp.int32, (ch, tk), 1)
           - lax.broadcasted_iota(jnp.int32, (ch, tk), 0))

    def key_tile(ref, j):
        return ref[pl.ds(pl.multiple_of(j * tk, tk), tk), :]

    def z_stage(slot, j, r_lo):
        k = key_tile(k_ref, j)
        for h in range(2):
            z_ref[slot, h, r_lo:, :] = _dot_nt(qm_ref[h, r_lo:, :], k)

    def p_stage(slot, h, off, r_lo):
        for r0 in range(r_lo, tq, ch):
            rows = slice(r0, r0 + ch)
            t = z_ref[slot, h, rows, :] * _LOG2E
            p = jnp.maximum(t, 0.0) + jnp.log2(1.0 + jnp.exp2(-jnp.abs(t)))
            car = car_ref[h, rows, :]
            tc = t + jnp.concatenate([car, car], axis=1)
            if off is not None:
                causal = cmr < off + r0
                p = jnp.where(causal, p, 0.0)
                tc = jnp.where(causal, tc, _SB_MASKED)
            pb_ref[h, rows, :] = p.astype(BF16)
            z_ref[slot, h, rows, :] = tc
            car_ref[h, rows, :] = car - jnp.sum(p, axis=-1, keepdims=True)

    def sum_stage(h, r_lo):
        bt_ref[h, r_lo:, :] = _dot(pb_ref[h, r_lo:, :], nu_ref[...])

    def w_stage(slot, h, r_lo):
        for r0 in range(r_lo, tq, ch):
            rows = slice(r0, r0 + ch)
            a_ref[h, rows, :] = jnp.exp2(z_ref[slot, h, rows, :] + bt_ref[h, rows, :]).astype(BF16)

    def pv_stage(j, r_lo):
        v = key_tile(v_ref, j)
        for h in range(2):
            acc_ref[h, r_lo:, :] += _dot(a_ref[h, r_lo:, :], v)

    def body(j, slot, off, r_lo=0, r_prev=0, r_next=0):
        if r_prev is not None:
            pv_stage(j + 1, r_prev)
        z_stage(1 - slot, jnp.maximum(j - 1, 0), r_next)
        p_stage(slot, 0, off, r_lo)
        sum_stage(0, r_lo)
        p_stage(slot, 1, off, r_lo)
        sum_stage(1, r_lo)
        w_stage(slot, 0, r_lo)
        w_stage(slot, 1, r_lo)

    assert ntile % 2 == 0
    first_row = [(ntile - 1 - i) * tk for i in range(ntile)] + [0]
    z_stage(0, last, first_row[0])
    for i in range(ntile):
        body(last - i, i % 2, (i + 1 - ntile) * tk, first_row[i],
             first_row[i - 1] if i else None, first_row[i + 1])

    @pl.loop(0, qb * (ntile // 2))
    def _(m):
        j = qb * ntile - 1 - 2 * m
        body(j, 0, None)
        body(j - 1, 1, None)

    pv_stage(0, 0)

    ys = []
    for h, half in enumerate([lane < HEAD_DIM, lane >= HEAD_DIM]):
        o = acc_ref[h]
        ms = jnp.sum(jnp.where(half, o * o, 0.0), axis=-1, keepdims=True) / HEAD_DIM
        ys.append(o * lax.rsqrt(ms + RMS_EPS))
    o_ref[...] = (jnp.where(lane < HEAD_DIM, ys[0], ys[1]) * g_ref[...]).astype(o_ref.dtype)


def _sb_attention(qkv, g_sb):
    b, s, _ = qkv.shape
    npair = D_SB // LANES
    tq, tk = _SB_TQ, _SB_TK
    r = lax.broadcasted_iota(jnp.int32, (tk, tk), 0)
    c = lax.broadcasted_iota(jnp.int32, (tk, tk), 1)
    nu = -(r >= c).astype(BF16)
    return pl.pallas_call(
        _sb_kernel,
        grid=(b, npair, s // tq),
        in_specs=[pl.BlockSpec((None, tq, LANES), lambda bi, p, i: (bi, i, _QB_BLK + p)),
                  pl.BlockSpec((None, s, LANES), lambda bi, p, i: (bi, 0, _KB_BLK + p)),
                  pl.BlockSpec((None, s, LANES), lambda bi, p, i: (bi, 0, _VB_BLK + p)),
                  pl.BlockSpec((tk, tk), lambda bi, p, i: (0, 0)),
                  pl.BlockSpec((1, LANES), lambda bi, p, i: (0, p))],
        out_specs=pl.BlockSpec((None, tq, LANES), lambda bi, p, i: (bi, i, p)),
        out_shape=jax.ShapeDtypeStruct((b, s, D_SB), BF16),
        scratch_shapes=[pltpu.VMEM((2, tq, LANES), BF16),
                        pltpu.VMEM((2, 2, tq, tk), F32),
                        pltpu.VMEM((2, tq, tk), BF16),
                        pltpu.VMEM((2, tq, tk), F32),
                        pltpu.VMEM((2, tq, tk), BF16),
                        pltpu.VMEM((2, tq, LANES), F32),
                        pltpu.VMEM((2, tq, LANES), F32)],
        compiler_params=_params(("parallel", "parallel", "parallel"), 32 << 20),
        name="sb_attention",
    )(qkv, qkv, qkv, nu, g_sb)


_LN_GROUPS = 4


def _proj_ln_q_kernel(oa_ref, ob_ref, wo_ref, x_ref, g_ref, b_ref, wq_ref, h_ref, q_ref,
                      mix_ref, hb_ref):
    bm = x_ref.shape[0]
    groups = [slice(r, r + bm // _LN_GROUPS) for r in range(0, bm, bm // _LN_GROUPS)]
    for rows in groups:
        mix_ref[rows, :] = (_dot(oa_ref[rows, :], wo_ref[:D_SWA, :])
                            + _dot(ob_ref[rows, :], wo_ref[D_SWA:, :]))
    for rows in groups:
        h = _layer_norm(ALPHA * x_ref[rows, :] + mix_ref[rows, :], g_ref[...], b_ref[...])
        h_ref[rows, :] = h
        hb_ref[rows, :] = h.astype(BF16)
    for rows in groups:
        q_ref[rows, :] = _dot(hb_ref[rows, :], wq_ref[...]).astype(BF16)


def _proj_ln_q(o_a, o_b, w_o, x, g, b, w_q, bm):
    m, d = x.shape
    row = lambda w: pl.BlockSpec((bm, w), lambda i: (i, 0))
    full = lambda r, w: pl.BlockSpec((r, w), lambda i: (0, 0), pipeline_mode=pl.Buffered(1))
    vmem = (2 * (2 * bm * D_SWA * 2 + bm * d * (4 + 4 + 2)) + 2 * d * d * 2
            + bm * d * (4 + 2) + 2 * bm * d * 4)
    return pl.pallas_call(
        _proj_ln_q_kernel,
        grid=(m // bm,),
        in_specs=[row(D_SWA), row(D_SB), full(d, d), row(d), full(1, d), full(1, d), full(d, d)],
        out_specs=[row(d), row(d)],
        out_shape=[jax.ShapeDtypeStruct((m, d), F32), jax.ShapeDtypeStruct((m, d), BF16)],
        scratch_shapes=[pltpu.VMEM((bm, d), F32), pltpu.VMEM((bm, d), BF16)],
        compiler_params=_params(("parallel",), vmem),
        name="mix_proj_ln_q",
    )(o_a, o_b, w_o, x, g, b, w_q)


_XATTN_CHUNK = 32


def _xattn_kernel(q_ref, kv_ref, w_ref, h_ref, g_ref, b_ref, o_ref, ob_ref,
                  s_ref, p_ref, rl_ref, c_ref, y_ref):
    bm = q_ref.shape[0]
    ch = _XATTN_CHUNK
    scale = 1.0 / math.sqrt(HEAD_DIM_MEM)
    heads = [slice(hd * HEAD_DIM_MEM, (hd + 1) * HEAD_DIM_MEM) for hd in range(N_HEADS_MEM)]
    for hd, cols in enumerate(heads):
        s_ref[hd] = _dot_nt(q_ref[:, cols], kv_ref[:, cols])
    for hd, cols in enumerate(heads):
        for r0 in range(0, bm, ch):
            rows = slice(r0, r0 + ch)
            s = s_ref[hd, rows, :] * scale
            p = jnp.exp(s - jnp.max(s, axis=-1, keepdims=True))
            l = jnp.sum(p, axis=-1, keepdims=True)
            p_ref[hd, rows, :] = p.astype(BF16)
            rl_ref[hd, rows, :] = jnp.broadcast_to(1.0 / l, (ch, LANES))
        o = _dot(p_ref[hd], kv_ref[:, D_MODEL + cols.start:D_MODEL + cols.stop])
        rl = rl_ref[hd]
        c_ref[:, cols] = (o * jnp.concatenate([rl] * (HEAD_DIM_MEM // LANES), axis=1)).astype(BF16)
    groups = [slice(r, r + bm // _LN_GROUPS) for r in range(0, bm, bm // _LN_GROUPS)]
    for rows in groups:
        y_ref[rows, :] = _dot(c_ref[rows, :], w_ref[...])
    for rows in groups:
        h = _layer_norm(ALPHA * h_ref[rows, :] + y_ref[rows, :], g_ref[...], b_ref[...])
        o_ref[rows, :] = h
        ob_ref[rows, :] = h.astype(BF16)


def _xattn_ln(q, kv, w_o, h, g, b, bm):
    bsz, s, d = q.shape
    mlen = kv.shape[1]
    row = lambda: pl.BlockSpec((None, bm, d), lambda bi, i: (bi, i, 0))
    full = lambda r, w: pl.BlockSpec((r, w), lambda bi, i: (0, 0), pipeline_mode=pl.Buffered(1))
    vmem = (2 * (bm * d * 2 + mlen * 2 * d * 2 + bm * d * (4 + 4 + 2)) + d * d * 2
            + N_HEADS_MEM * bm * (mlen * 6 + LANES * 4) + bm * d * (2 + 4) + 2 * bm * d * 4)
    return pl.pallas_call(
        _xattn_kernel,
        grid=(bsz, s // bm),
        in_specs=[row(), pl.BlockSpec((None, mlen, 2 * d), lambda bi, i: (bi, 0, 0)),
                  full(d, d), row(), full(1, d), full(1, d)],
        out_specs=[row(), row()],
        out_shape=[jax.ShapeDtypeStruct((bsz, s, d), F32),
                   jax.ShapeDtypeStruct((bsz, s, d), BF16)],
        scratch_shapes=[pltpu.VMEM((N_HEADS_MEM, bm, mlen), F32),
                        pltpu.VMEM((N_HEADS_MEM, bm, mlen), BF16),
                        pltpu.VMEM((N_HEADS_MEM, bm, LANES), F32),
                        pltpu.VMEM((bm, d), BF16),
                        pltpu.VMEM((bm, d), F32)],
        compiler_params=_params(("parallel", "parallel"), vmem),
        name="mem_xattn_ln",
    )(q, kv, w_o, h, g, b)


def _ffn_kernel(xb_ref, wg_ref, wu_ref, wd_ref, h_ref, g_ref, b_ref, o_ref, acc_ref):
    j = pl.program_id(1)

    @pl.when(j == 0)
    def _():
        acc_ref[...] = jnp.zeros_like(acc_ref)

    x = xb_ref[...]
    gate = _dot(x, wg_ref[...])
    up = _dot(x, wu_ref[...])
    act = gate / (1.0 + jnp.exp(-gate)) * up
    acc_ref[...] += _dot(act.astype(BF16), wd_ref[...])

    @pl.when(j == pl.num_programs(1) - 1)
    def _():
        o_ref[...] = _layer_norm(ALPHA * h_ref[...] + acc_ref[...], g_ref[...], b_ref[...])


def _ffn_ln(hb, w_gu, w_d, h, g, b, bm, bf):
    m, d = h.shape
    dff = w_d.shape[0]
    nf = dff // bf
    assert dff % bf == 0 and m % bm == 0
    row = lambda: pl.BlockSpec((bm, d), lambda i, j: (i, 0))
    full = lambda: pl.BlockSpec((1, d), lambda i, j: (0, 0))
    vmem = 2 * (bm * d * 2 + 3 * d * bf * 2 + 2 * bm * d * 4) + bm * d * 4 + 4 * bm * bf * 4
    return pl.pallas_call(
        _ffn_kernel,
        grid=(m // bm, nf),
        in_specs=[row(),
                  pl.BlockSpec((d, bf), lambda i, j: (0, j)),
                  pl.BlockSpec((d, bf), lambda i, j: (0, j + nf)),
                  pl.BlockSpec((bf, d), lambda i, j: (j, 0)),
                  row(), full(), full()],
        out_specs=row(),
        out_shape=jax.ShapeDtypeStruct((m, d), F32),
        scratch_shapes=[pltpu.VMEM((bm, d), F32)],
        compiler_params=_params(("parallel", "arbitrary"), vmem + (4 << 20)),
        name="swiglu_ffn_ln",
    )(hb, w_gu, w_gu, w_d, h, g, b)


def kernel(x, mem, w_in, sinks, g_swa, g_sb, w_o, ln1_g, ln1_b, w_q_mem, w_kv_mem,
           w_o_mem, ln2_g, ln2_b, w_gate_up, w_down, ln3_g, ln3_b):
    bsz, s, d = x.shape
    t = bsz * s
    mlen = mem.shape[1]
    h = x.reshape(t, d)
    for l in range(DEPTH):
        row = lambda a: a[l].reshape(1, -1)
        qkv = _matmul(h, w_in[l].astype(BF16), 1024, 1536, BF16, "qkv_proj")
        qkv = qkv.reshape(bsz, s, D_IN)
        o_a = _swa_attention(qkv, sinks[l], row(g_swa))
        o_b = _sb_attention(qkv, row(g_sb))
        h, q = _proj_ln_q(o_a.reshape(t, D_SWA), o_b.reshape(t, D_SB), w_o[l].astype(BF16),
                          h, row(ln1_g), row(ln1_b), w_q_mem[l].astype(BF16), 512)
        kv = _matmul(mem.reshape(bsz * mlen, d), w_kv_mem[l].astype(BF16),
                     1024, 1024, BF16, "mem_kv_proj")
        h3, hb3 = _xattn_ln(q.reshape(bsz, s, d), kv.reshape(bsz, mlen, 2 * d),
                            w_o_mem[l].astype(BF16), h.reshape(bsz, s, d),
                            row(ln2_g), row(ln2_b), 512)
        h, hb = h3.reshape(t, d), hb3.reshape(t, d)
        h = _ffn_ln(hb, w_gate_up[l].astype(BF16), w_down[l].astype(BF16), h,
                    row(ln3_g), row(ln3_b), 512, 512)
    return h.reshape(bsz, s, d)
```

```python
import functools
import math

import jax
import jax.numpy as jnp
from jax import lax
from jax.experimental import pallas as pl
from jax.experimental.pallas import tpu as pltpu

F32 = jnp.float32
BF16 = jnp.bfloat16

D_MODEL = 2048
HEAD_DIM = 64
N_HEADS_SWA = 16
N_KV_SWA = 4
N_HEADS_SB = 16
WINDOW = 128
BLOCK = 128
N_HEADS_MEM = 4
HEAD_DIM_MEM = D_MODEL // N_HEADS_MEM
D_SWA = N_HEADS_SWA * HEAD_DIM
D_KV_SWA = N_KV_SWA * HEAD_DIM
D_SB = N_HEADS_SB * HEAD_DIM
D_IN = D_SWA + 2 * D_KV_SWA + 3 * D_SB
DEPTH = 1
ALPHA = (2.0 * DEPTH) ** 0.25
LN_EPS = 1e-5
RMS_EPS = 1e-6
QK_SCALE = 1.0 / math.sqrt(HEAD_DIM)

LANES = 128
V7X_VMEM_LIMIT = 56 * 1024 * 1024

_KA_BLK = D_SWA // LANES
_VA_BLK = (D_SWA + D_KV_SWA) // LANES
_QB_BLK = (D_SWA + 2 * D_KV_SWA) // LANES
_KB_BLK = _QB_BLK + D_SB // LANES
_VB_BLK = _KB_BLK + D_SB // LANES


def _params(sem, vmem_bytes, flags=None):
    return pltpu.CompilerParams(
        dimension_semantics=sem, vmem_limit_bytes=min(int(vmem_bytes), V7X_VMEM_LIMIT),
        flags=flags)


def _dot(a, b):
    return jnp.dot(a, b, preferred_element_type=F32)


def _dot_nt(a, b):
    return lax.dot_general(a, b, (((1,), (1,)), ((), ())), preferred_element_type=F32)


def _layer_norm(y, g, b):
    mu = jnp.mean(y, axis=-1, keepdims=True)
    d = y - mu
    var = jnp.mean(d * d, axis=-1, keepdims=True)
    return d * lax.rsqrt(var + LN_EPS) * g + b


def _mm_kernel(x_ref, w_ref, o_ref, *xb_ref):
    if xb_ref:
        @pl.when(pl.program_id(1) == 0)
        def _():
            xb_ref[0][...] = x_ref[...].astype(BF16)
        x = xb_ref[0][...]
    else:
        x = x_ref[...]
    o_ref[...] = _dot(x, w_ref[...]).astype(o_ref.dtype)


def _matmul(x, w, bm, bn, out_dtype, name):
    m, k = x.shape
    n = w.shape[1]
    assert m % bm == 0 and n % bn == 0
    osz = jnp.dtype(out_dtype).itemsize
    xsz = jnp.dtype(x.dtype).itemsize
    cast = x.dtype != BF16
    vmem = 2 * (bm * k * xsz + k * bn * 2 + bm * bn * osz) + bm * bn * 4 + cast * bm * k * 2
    return pl.pallas_call(
        _mm_kernel,
        grid=(m // bm, n // bn),
        in_specs=[pl.BlockSpec((bm, k), lambda i, j: (i, 0)),
                  pl.BlockSpec((k, bn), lambda i, j: (0, j))],
        out_specs=pl.BlockSpec((bm, bn), lambda i, j: (i, j)),
        out_shape=jax.ShapeDtypeStruct((m, n), out_dtype),
        scratch_shapes=[pltpu.VMEM((bm, k), BF16)] if cast else [],
        compiler_params=_params(("parallel", "arbitrary"), vmem + (8 << 20)),
        name=name,
    )(x, w)


def _alibi_slope(h):
    return 2.0 ** (-8.0 * (h + 1) / N_HEADS_SWA)


_SWA_CHUNK = 32


def _swa_kernel(sink_ref, q_ref, kp_ref, kc_ref, vp_ref, vc_ref, g_ref, o_ref,
                qs_ref, k_s, vk_ref, s_ref, p_ref, rl_ref, pv_ref):
    kvp = pl.program_id(1)
    n = pl.program_id(2)
    ch = _SWA_CHUNK
    k_s[:BLOCK] = kp_ref[...]
    k_s[BLOCK:] = kc_ref[...]
    lane = lax.broadcasted_iota(jnp.int32, (BLOCK, LANES), 1)
    halves = [lane < HEAD_DIM, lane >= HEAD_DIM]

    def stack_heads(c):
        return [(2 * c, c), (2 * c + 1, c), (2 * c, 1 - c), (2 * c + 1, 1 - c)]

    for t in range(4):
        c = t // 2
        q_t = q_ref[:, t * LANES:(t + 1) * LANES]
        q_sw = pltpu.roll(q_t.astype(F32), HEAD_DIM, 1).astype(BF16)
        for blk, (tt, u) in enumerate(stack_heads(c)):
            if tt == t:
                src = q_t if u == c else q_sw
                qs_ref[c, blk * BLOCK:(blk + 1) * BLOCK, :] = (
                    jnp.where(halves[c], src, jnp.zeros_like(src)) * QK_SCALE)

    for c in range(2):
        s_ref[c] = _dot_nt(qs_ref[c], k_s[...])

    v = jnp.concatenate([vp_ref[...], vc_ref[...]], axis=0).astype(F32)
    v_sw = pltpu.roll(v, HEAD_DIM, 1)
    lane2 = lax.broadcasted_iota(jnp.int32, (2 * BLOCK, LANES), 1)
    for c in range(2):
        in_c = (lane2 >= HEAD_DIM) if c else (lane2 < HEAD_DIM)
        vk_ref[c, 0] = jnp.where(in_c, v, 0.0).astype(BF16)
        vk_ref[c, 1] = jnp.where(in_c, 0.0, v_sw).astype(BF16)

    qi = lax.broadcasted_iota(jnp.int32, (ch, 2 * BLOCK), 0)
    kj = lax.broadcasted_iota(jnp.int32, (ch, 2 * BLOCK), 1)
    pens = []
    for r0 in range(0, BLOCK, ch):
        dist = qi + (r0 + BLOCK) - kj
        valid = (dist >= 0) & (dist < WINDOW) & ((kj >= BLOCK) | (n > 0))
        pens.append(jnp.where(valid, dist.astype(F32), jnp.inf))

    for c in range(2):
        for blk, (t, u) in enumerate(stack_heads(c)):
            slope = jnp.where(kvp == 0, _alibi_slope(t * 2 + u), _alibi_slope(8 + t * 2 + u))
            sink = sink_ref[kvp * 8 + t * 2 + u]
            for i, pen in enumerate(pens):
                r0 = blk * BLOCK + i * ch
                rows = slice(r0, r0 + ch)
                s = s_ref[c, rows, :] - slope * pen
                m = jnp.maximum(jnp.max(s, axis=-1, keepdims=True), sink)
                p = jnp.exp(s - m)
                den = jnp.sum(p, axis=-1, keepdims=True) + jnp.exp(sink - m)
                p_ref[c, rows, :] = p.astype(BF16)
                rl_ref[c, rows, :] = jnp.broadcast_to(1.0 / den, (ch, LANES))
        half_rows = 2 * BLOCK
        pv_ref[c, :half_rows, :] = _dot(p_ref[c, :half_rows, :], vk_ref[c, 0])
        pv_ref[c, half_rows:, :] = _dot(p_ref[c, half_rows:, :], vk_ref[c, 1])

        for i in range(2):
            t = 2 * c + i
            y2 = None
            for blk in (i, 2 + i):
                rows = slice(blk * BLOCK, (blk + 1) * BLOCK)
                pv = pv_ref[c, rows, :]
                rl = rl_ref[c, rows, :]
                ss = jnp.sum(pv * pv, axis=-1, keepdims=True)
                y = pv * (rl * lax.rsqrt(rl * rl * ss / HEAD_DIM + RMS_EPS))
                y2 = y if y2 is None else y2 + y
            o_ref[:, t * LANES:(t + 1) * LANES] = (
                y2 * g_ref[:, t * LANES:(t + 1) * LANES]).astype(o_ref.dtype)


def _swa_attention(qkv, sinks, g_swa):
    b, s, _ = qkv.shape
    nb = s // BLOCK
    qw = 4 * LANES
    kv_spec = lambda blk, prev: pl.BlockSpec(
        (None, BLOCK, LANES),
        (lambda bi, kp, n: (bi, jnp.maximum(n - 1, 0), blk + kp)) if prev
        else (lambda bi, kp, n: (bi, n, blk + kp)))
    return pl.pallas_call(
        _swa_kernel,
        grid=(b, 2, nb),
        in_specs=[pl.BlockSpec(memory_space=pltpu.SMEM),
                  pl.BlockSpec((None, BLOCK, qw), lambda bi, kp, n: (bi, n, kp)),
                  kv_spec(_KA_BLK, True), kv_spec(_KA_BLK, False),
                  kv_spec(_VA_BLK, True), kv_spec(_VA_BLK, False),
                  pl.BlockSpec((1, qw), lambda bi, kp, n: (0, kp))],
        out_specs=pl.BlockSpec((None, BLOCK, qw), lambda bi, kp, n: (bi, n, kp)),
        out_shape=jax.ShapeDtypeStruct((b, s, D_SWA), BF16),
        scratch_shapes=[pltpu.VMEM((2, 4 * BLOCK, LANES), BF16),
                        pltpu.VMEM((2 * BLOCK, LANES), BF16),
                        pltpu.VMEM((2, 2, 2 * BLOCK, LANES), BF16),
                        pltpu.VMEM((2, 4 * BLOCK, 2 * BLOCK), F32),
                        pltpu.VMEM((2, 4 * BLOCK, 2 * BLOCK), BF16),
                        pltpu.VMEM((2, 4 * BLOCK, LANES), F32),
                        pltpu.VMEM((2, 4 * BLOCK, LANES), F32)],
        compiler_params=_params(("parallel", "parallel", "parallel"), 16 << 20),
        name="swa_attention",
    )(sinks, qkv, qkv, qkv, qkv, qkv, g_swa)


_SB_TK = 256
_SB_TQ = 512
_SB_CHUNK = 32
_SB_MASKED = -1e30
_LOG2E = 1.4426950408889634


def _sb_kernel(q_ref, k_ref, v_ref, nu_ref, g_ref, o_ref,
               qm_ref, z_ref, pb_ref, bt_ref, a_ref, acc_ref, car_ref):
    qb = pl.program_id(2)
    tq, tk, ch = _SB_TQ, _SB_TK, _SB_CHUNK
    ntile = tq // tk
    last = qb * ntile + ntile - 1
    lane = lax.broadcasted_iota(jnp.int32, (tq, LANES), 1)
    q = q_ref[...] * QK_SCALE
    qm_ref[0] = jnp.where(lane < HEAD_DIM, q, jnp.zeros_like(q))
    qm_ref[1] = jnp.where(lane >= HEAD_DIM, q, jnp.zeros_like(q))
    acc_ref[...] = jnp.zeros_like(acc_ref)
    car_ref[...] = jnp.zeros_like(car_ref)
    cmr = (lax.broadcasted_iota(jnp.int32, (ch, tk), 1)
           - lax.broadcasted_iota(jnp.int32, (ch, tk), 0))

    def key_tile(ref, j):
        return ref[pl.ds(pl.multiple_of(j * tk, tk), tk), :]

    def z_stage(slot, j, r_lo):
        k = key_tile(k_ref, j)
        for h in range(2):
            z_ref[slot, h, r_lo:, :] = _dot_nt(qm_ref[h, r_lo:, :], k)

    def p_stage(slot, h, off, r_lo):
        for r0 in range(r_lo, tq, ch):
            rows = slice(r0, r0 + ch)
            t = z_ref[slot, h, rows, :] * _LOG2E
            p = jnp.maximum(t, 0.0) + jnp.log2(1.0 + jnp.exp2(-jnp.abs(t)))
            car = car_ref[h, rows, :]
            tc = t + jnp.concatenate([car, car], axis=1)
            if off is not None:
                causal = cmr < off + r0
                p = jnp.where(causal, p, 0.0)
                tc = jnp.where(causal, tc, _SB_MASKED)
            pb_ref[h, rows, :] = p.astype(BF16)
            z_ref[slot, h, rows, :] = tc
            car_ref[h, rows, :] = car - jnp.sum(p, axis=-1, keepdims=True)

    def sum_stage(h, r_lo):
        bt_ref[h, r_lo:, :] = _dot(pb_ref[h, r_lo:, :], nu_ref[...])

    def w_stage(slot, h, r_lo):
        for r0 in range(r_lo, tq, ch):
            rows = slice(r0, r0 + ch)
            a_ref[h, rows, :] = jnp.exp2(z_ref[slot, h, rows, :] + bt_ref[h, rows, :]).astype(BF16)

    def pv_stage(j, r_lo):
        v = key_tile(v_ref, j)
        for h in range(2):
            acc_ref[h, r_lo:, :] += _dot(a_ref[h, r_lo:, :], v)

    def body(j, slot, off, r_lo=0, r_prev=0, r_next=0):
        if r_prev is not None:
            pv_stage(j + 1, r_prev)
        z_stage(1 - slot, jnp.maximum(j - 1, 0), r_next)
        p_stage(slot, 0, off, r_lo)
        sum_stage(0, r_lo)
        p_stage(slot, 1, off, r_lo)
        sum_stage(1, r_lo)
        w_stage(slot, 0, r_lo)
        w_stage(slot, 1, r_lo)

    assert ntile % 2 == 0
    first_row = [(ntile - 1 - i) * tk for i in range(ntile)] + [0]
    z_stage(0, last, first_row[0])
    for i in range(ntile):
        body(last - i, i % 2, (i + 1 - ntile) * tk, first_row[i],
             first_row[i - 1] if i else None, first_row[i + 1])

    @pl.loop(0, qb * (ntile // 2))
    def _(m):
        j = qb * ntile - 1 - 2 * m
        body(j, 0, None)
        body(j - 1, 1, None)

    pv_stage(0, 0)

    ys = []
    for h, half in enumerate([lane < HEAD_DIM, lane >= HEAD_DIM]):
        o = acc_ref[h]
        ms = jnp.sum(jnp.where(half, o * o, 0.0), axis=-1, keepdims=True) / HEAD_DIM
        ys.append(o * lax.rsqrt(ms + RMS_EPS))
    o_ref[...] = (jnp.where(lane < HEAD_DIM, ys[0], ys[1]) * g_ref[...]).astype(o_ref.dtype)


def _sb_attention(qkv, g_sb):
    b, s, _ = qkv.shape
    npair = D_SB // LANES
    tq, tk = _SB_TQ, _SB_TK
    r = lax.broadcasted_iota(jnp.int32, (tk, tk), 0)
    c = lax.broadcasted_iota(jnp.int32, (tk, tk), 1)
    nu = -(r >= c).astype(BF16)
    return pl.pallas_call(
        _sb_kernel,
        grid=(b, npair, s // tq),
        in_specs=[pl.BlockSpec((None, tq, LANES), lambda bi, p, i: (bi, i, _QB_BLK + p)),
                  pl.BlockSpec((None, s, LANES), lambda bi, p, i: (bi, 0, _KB_BLK + p)),
                  pl.BlockSpec((None, s, LANES), lambda bi, p, i: (bi, 0, _VB_BLK + p)),
                  pl.BlockSpec((tk, tk), lambda bi, p, i: (0, 0)),
                  pl.BlockSpec((1, LANES), lambda bi, p, i: (0, p))],
        out_specs=pl.BlockSpec((None, tq, LANES), lambda bi, p, i: (bi, i, p)),
        out_shape=jax.ShapeDtypeStruct((b, s, D_SB), BF16),
        scratch_shapes=[pltpu.VMEM((2, tq, LANES), BF16),
                        pltpu.VMEM((2, 2, tq, tk), F32),
                        pltpu.VMEM((2, tq, tk), BF16),
                        pltpu.VMEM((2, tq, tk), F32),
                        pltpu.VMEM((2, tq, tk), BF16),
                        pltpu.VMEM((2, tq, LANES), F32),
                        pltpu.VMEM((2, tq, LANES), F32)],
        compiler_params=_params(("parallel", "parallel", "parallel"), 32 << 20),
        name="sb_attention",
    )(qkv, qkv, qkv, nu, g_sb)


_LN_GROUPS = 4


def _proj_ln_q_kernel(oa_ref, ob_ref, wo_ref, x_ref, g_ref, b_ref, wq_ref, h_ref, q_ref,
                      mix_ref, hb_ref):
    bm = x_ref.shape[0]
    groups = [slice(r, r + bm // _LN_GROUPS) for r in range(0, bm, bm // _LN_GROUPS)]
    for rows in groups:
        mix_ref[rows, :] = (_dot(oa_ref[rows, :], wo_ref[:D_SWA, :])
                            + _dot(ob_ref[rows, :], wo_ref[D_SWA:, :]))
    for rows in groups:
        h = _layer_norm(ALPHA * x_ref[rows, :] + mix_ref[rows, :], g_ref[...], b_ref[...])
        h_ref[rows, :] = h
        hb_ref[rows, :] = h.astype(BF16)
    for rows in groups:
        q_ref[rows, :] = _dot(hb_ref[rows, :], wq_ref[...]).astype(BF16)


def _proj_ln_q(o_a, o_b, w_o, x, g, b, w_q, bm):
    m, d = x.shape
    row = lambda w: pl.BlockSpec((bm, w), lambda i: (i, 0))
    full = lambda r, w: pl.BlockSpec((r, w), lambda i: (0, 0), pipeline_mode=pl.Buffered(1))
    vmem = (2 * (2 * bm * D_SWA * 2 + bm * d * (4 + 4 + 2)) + 2 * d * d * 2
            + bm * d * (4 + 2) + 2 * bm * d * 4)
    return pl.pallas_call(
        _proj_ln_q_kernel,
        grid=(m // bm,),
        in_specs=[row(D_SWA), row(D_SB), full(d, d), row(d), full(1, d), full(1, d), full(d, d)],
        out_specs=[row(d), row(d)],
        out_shape=[jax.ShapeDtypeStruct((m, d), F32), jax.ShapeDtypeStruct((m, d), BF16)],
        scratch_shapes=[pltpu.VMEM((bm, d), F32), pltpu.VMEM((bm, d), BF16)],
        compiler_params=_params(("parallel",), vmem),
        name="mix_proj_ln_q",
    )(o_a, o_b, w_o, x, g, b, w_q)


_XATTN_CHUNK = 32


def _xattn_kernel(q_ref, kv_ref, w_ref, h_ref, g_ref, b_ref, o_ref, ob_ref,
                  s_ref, p_ref, rl_ref, c_ref, y_ref):
    bm = q_ref.shape[0]
    ch = _XATTN_CHUNK
    scale = 1.0 / math.sqrt(HEAD_DIM_MEM)
    heads = [slice(hd * HEAD_DIM_MEM, (hd + 1) * HEAD_DIM_MEM) for hd in range(N_HEADS_MEM)]
    for hd, cols in enumerate(heads):
        s_ref[hd] = _dot_nt(q_ref[:, cols], kv_ref[:, cols])
    for hd, cols in enumerate(heads):
        for r0 in range(0, bm, ch):
            rows = slice(r0, r0 + ch)
            s = s_ref[hd, rows, :] * scale
            p = jnp.exp(s - jnp.max(s, axis=-1, keepdims=True))
            l = jnp.sum(p, axis=-1, keepdims=True)
            p_ref[hd, rows, :] = p.astype(BF16)
            rl_ref[hd, rows, :] = jnp.broadcast_to(1.0 / l, (ch, LANES))
        o = _dot(p_ref[hd], kv_ref[:, D_MODEL + cols.start:D_MODEL + cols.stop])
        rl = rl_ref[hd]
        c_ref[:, cols] = (o * jnp.concatenate([rl] * (HEAD_DIM_MEM // LANES), axis=1)).astype(BF16)
    groups = [slice(r, r + bm // _LN_GROUPS) for r in range(0, bm, bm // _LN_GROUPS)]
    for rows in groups:
        y_ref[rows, :] = _dot(c_ref[rows, :], w_ref[...])
    for rows in groups:
        h = _layer_norm(ALPHA * h_ref[rows, :] + y_ref[rows, :], g_ref[...], b_ref[...])
        o_ref[rows, :] = h
        ob_ref[rows, :] = h.astype(BF16)


def _xattn_ln(q, kv, w_o, h, g, b, bm):
    bsz, s, d = q.shape
    mlen = kv.shape[1]
    row = lambda: pl.BlockSpec((None, bm, d), lambda bi, i: (bi, i, 0))
    full = lambda r, w: pl.BlockSpec((r, w), lambda bi, i: (0, 0), pipeline_mode=pl.Buffered(1))
    vmem = (2 * (bm * d * 2 + mlen * 2 * d * 2 + bm * d * (4 + 4 + 2)) + d * d * 2
            + N_HEADS_MEM * bm * (mlen * 6 + LANES * 4) + bm * d * (2 + 4) + 2 * bm * d * 4)
    return pl.pallas_call(
        _xattn_kernel,
        grid=(bsz, s // bm),
        in_specs=[row(), pl.BlockSpec((None, mlen, 2 * d), lambda bi, i: (bi, 0, 0)),
                  full(d, d), row(), full(1, d), full(1, d)],
        out_specs=[row(), row()],
        out_shape=[jax.ShapeDtypeStruct((bsz, s, d), F32),
                   jax.ShapeDtypeStruct((bsz, s, d), BF16)],
        scratch_shapes=[pltpu.VMEM((N_HEADS_MEM, bm, mlen), F32),
                        pltpu.VMEM((N_HEADS_MEM, bm, mlen), BF16),
                        pltpu.VMEM((N_HEADS_MEM, bm, LANES), F32),
                        pltpu.VMEM((bm, d), BF16),
                        pltpu.VMEM((bm, d), F32)],
        compiler_params=_params(("parallel", "parallel"), vmem),
        name="mem_xattn_ln",
    )(q, kv, w_o, h, g, b)


def _ffn_kernel(xb_ref, wg_ref, wu_ref, wd_ref, h_ref, g_ref, b_ref, o_ref, acc_ref):
    j = pl.program_id(1)

    @pl.when(j == 0)
    def _():
        acc_ref[...] = jnp.zeros_like(acc_ref)

    x = xb_ref[...]
    gate = _dot(x, wg_ref[...])
    up = _dot(x, wu_ref[...])
    act = gate / (1.0 + jnp.exp(-gate)) * up
    acc_ref[...] += _dot(act.astype(BF16), wd_ref[...])

    @pl.when(j == pl.num_programs(1) - 1)
    def _():
        o_ref[...] = _layer_norm(ALPHA * h_ref[...] + acc_ref[...], g_ref[...], b_ref[...])


def _ffn_ln(hb, w_gu, w_d, h, g, b, bm, bf):
    m, d = h.shape
    dff = w_d.shape[0]
    nf = dff // bf
    assert dff % bf == 0 and m % bm == 0
    row = lambda: pl.BlockSpec((bm, d), lambda i, j: (i, 0))
    full = lambda: pl.BlockSpec((1, d), lambda i, j: (0, 0))
    vmem = 2 * (bm * d * 2 + 3 * d * bf * 2 + 2 * bm * d * 4) + bm * d * 4 + 4 * bm * bf * 4
    return pl.pallas_call(
        _ffn_kernel,
        grid=(m // bm, nf),
        in_specs=[row(),
                  pl.BlockSpec((d, bf), lambda i, j: (0, j)),
                  pl.BlockSpec((d, bf), lambda i, j: (0, j + nf)),
                  pl.BlockSpec((bf, d), lambda i, j: (j, 0)),
                  row(), full(), full()],
        out_specs=row(),
        out_shape=jax.ShapeDtypeStruct((m, d), F32),
        scratch_shapes=[pltpu.VMEM((bm, d), F32)],
        compiler_params=_params(("parallel", "arbitrary"), vmem + (4 << 20)),
        name="swiglu_ffn_ln",
    )(hb, w_gu, w_gu, w_d, h, g, b)


def kernel(x, mem, w_in, sinks, g_swa, g_sb, w_o, ln1_g, ln1_b, w_q_mem, w_kv_mem,
           w_o_mem, ln2_g, ln2_b, w_gate_up, w_down, ln3_g, ln3_b):
    bsz, s, d = x.shape
    t = bsz * s
    mlen = mem.shape[1]
    h = x.reshape(t, d)
    for l in range(DEPTH):
        row = lambda a: a[l].reshape(1, -1)
        qkv = _matmul(h, w_in[l].astype(BF16), 1024, 1536, BF16, "qkv_proj")
        qkv = qkv.reshape(bsz, s, D_IN)
        o_a = _swa_attention(qkv, sinks[l], row(g_swa))
        o_b = _sb_attention(qkv, row(g_sb))
        h, q = _proj_ln_q(o_a.reshape(t, D_SWA), o_b.reshape(t, D_SB), w_o[l].astype(BF16),
                          h, row(ln1_g), row(ln1_b), w_q_mem[l].astype(BF16), 512)
        kv = _matmul(mem.reshape(bsz * mlen, d), w_kv_mem[l].astype(BF16),
                     1024, 1024, BF16, "mem_kv_proj")
        h3, hb3 = _xattn_ln(q.reshape(bsz, s, d), kv.reshape(bsz, mlen, 2 * d),
                            w_o_mem[l].astype(BF16), h.reshape(bsz, s, d),
                            row(ln2_g), row(ln2_b), 512)
        h, hb = h3.reshape(t, d), hb3.reshape(t, d)
        h = _ffn_ln(hb, w_gate_up[l].astype(BF16), w_down[l].astype(BF16), h,
                    row(ln3_g), row(ln3_b), 512, 512)
    return h.reshape(bsz, s, d)
```

```python
import functools
import math

import jax
import jax.numpy as jnp
from jax import lax
from jax.experimental import pallas as pl
from jax.experimental.pallas import tpu as pltpu

F32 = jnp.float32
BF16 = jnp.bfloat16

D_MODEL = 2048
HEAD_DIM = 64
N_HEADS_SWA = 16
N_KV_SWA = 4
N_HEADS_SB = 16
WINDOW = 128
BLOCK = 128
N_HEADS_MEM = 4
HEAD_DIM_MEM = D_MODEL // N_HEADS_MEM
D_SWA = N_HEADS_SWA * HEAD_DIM
D_KV_SWA = N_KV_SWA * HEAD_DIM
D_SB = N_HEADS_SB * HEAD_DIM
D_IN = D_SWA + 2 * D_KV_SWA + 3 * D_SB
DEPTH = 1
ALPHA = (2.0 * DEPTH) ** 0.25
LN_EPS = 1e-5
RMS_EPS = 1e-6
QK_SCALE = 1.0 / math.sqrt(HEAD_DIM)

LANES = 128
V7X_VMEM_LIMIT = 56 * 1024 * 1024

_KA_BLK = D_SWA // LANES
_VA_BLK = (D_SWA + D_KV_SWA) // LANES
_QB_BLK = (D_SWA + 2 * D_KV_SWA) // LANES
_KB_BLK = _QB_BLK + D_SB // LANES
_VB_BLK = _KB_BLK + D_SB // LANES


def _params(sem, vmem_bytes, flags=None):
    return pltpu.CompilerParams(
        dimension_semantics=sem, vmem_limit_bytes=min(int(vmem_bytes), V7X_VMEM_LIMIT),
        flags=flags)


def _dot(a, b):
    return jnp.dot(a, b, preferred_element_type=F32)


def _dot_nt(a, b):
    return lax.dot_general(a, b, (((1,), (1,)), ((), ())), preferred_element_type=F32)


def _layer_norm(y, g, b):
    mu = jnp.mean(y, axis=-1, keepdims=True)
    d = y - mu
    var = jnp.mean(d * d, axis=-1, keepdims=True)
    return d * lax.rsqrt(var + LN_EPS) * g + b


def _mm_kernel(x_ref, w_ref, o_ref, *xb_ref):
    if xb_ref:
        @pl.when(pl.program_id(1) == 0)
        def _():
            xb_ref[0][...] = x_ref[...].astype(BF16)
        x = xb_ref[0][...]
    else:
        x = x_ref[...]
    o_ref[...] = _dot(x, w_ref[...]).astype(o_ref.dtype)


def _matmul(x, w, bm, bn, out_dtype, name):
    m, k = x.shape
    n = w.shape[1]
    assert m % bm == 0 and n % bn == 0
    osz = jnp.dtype(out_dtype).itemsize
    xsz = jnp.dtype(x.dtype).itemsize
    cast = x.dtype != BF16
    vmem = 2 * (bm * k * xsz + k * bn * 2 + bm * bn * osz) + bm * bn * 4 + cast * bm * k * 2
    return pl.pallas_call(
        _mm_kernel,
        grid=(m // bm, n // bn),
        in_specs=[pl.BlockSpec((bm, k), lambda i, j: (i, 0)),
                  pl.BlockSpec((k, bn), lambda i, j: (0, j))],
        out_specs=pl.BlockSpec((bm, bn), lambda i, j: (i, j)),
        out_shape=jax.ShapeDtypeStruct((m, n), out_dtype),
        scratch_shapes=[pltpu.VMEM((bm, k), BF16)] if cast else [],
        compiler_params=_params(("parallel", "arbitrary"), vmem + (8 << 20)),
        name=name,
    )(x, w)


def _alibi_slope(h):
    return 2.0 ** (-8.0 * (h + 1) / N_HEADS_SWA)


_SWA_CHUNK = 32


def _swa_kernel(sink_ref, q_ref, kp_ref, kc_ref, vp_ref, vc_ref, g_ref, o_ref,
                qs_ref, k_s, vk_ref, s_ref, p_ref, rl_ref, pv_ref):
    kvp = pl.program_id(1)
    n = pl.program_id(2)
    ch = _SWA_CHUNK
    k_s[:BLOCK] = kp_ref[...]
    k_s[BLOCK:] = kc_ref[...]
    lane = lax.broadcasted_iota(jnp.int32, (BLOCK, LANES), 1)
    halves = [lane < HEAD_DIM, lane >= HEAD_DIM]

    def stack_heads(c):
        return [(2 * c, c), (2 * c + 1, c), (2 * c, 1 - c), (2 * c + 1, 1 - c)]

    for t in range(4):
        c = t // 2
        q_t = q_ref[:, t * LANES:(t + 1) * LANES]
        q_sw = pltpu.roll(q_t.astype(F32), HEAD_DIM, 1).astype(BF16)
        for blk, (tt, u) in enumerate(stack_heads(c)):
            if tt == t:
                src = q_t if u == c else q_sw
                qs_ref[c, blk * BLOCK:(blk + 1) * BLOCK, :] = (
                    jnp.where(halves[c], src, jnp.zeros_like(src)) * QK_SCALE)

    for c in range(2):
        s_ref[c] = _dot_nt(qs_ref[c], k_s[...])

    v = jnp.concatenate([vp_ref[...], vc_ref[...]], axis=0).astype(F32)
    v_sw = pltpu.roll(v, HEAD_DIM, 1)
    lane2 = lax.broadcasted_iota(jnp.int32, (2 * BLOCK, LANES), 1)
    for c in range(2):
        in_c = (lane2 >= HEAD_DIM) if c else (lane2 < HEAD_DIM)
        vk_ref[c, 0] = jnp.where(in_c, v, 0.0).astype(BF16)
        vk_ref[c, 1] = jnp.where(in_c, 0.0, v_sw).astype(BF16)

    qi = lax.broadcasted_iota(jnp.int32, (ch, 2 * BLOCK), 0)
    kj = lax.broadcasted_iota(jnp.int32, (ch, 2 * BLOCK), 1)
    pens = []
    for r0 in range(0, BLOCK, ch):
        dist = qi + (r0 + BLOCK) - kj
        valid = (dist >= 0) & (dist < WINDOW) & ((kj >= BLOCK) | (n > 0))
        pens.append(jnp.where(valid, dist.astype(F32), jnp.inf))

    for c in range(2):
        for blk, (t, u) in enumerate(stack_heads(c)):
            slope = jnp.where(kvp == 0, _alibi_slope(t * 2 + u), _alibi_slope(8 + t * 2 + u))
            sink = sink_ref[kvp * 8 + t * 2 + u]
            for i, pen in enumerate(pens):
                r0 = blk * BLOCK + i * ch
                rows = slice(r0, r0 + ch)
                s = s_ref[c, rows, :] - slope * pen
                m = jnp.maximum(jnp.max(s, axis=-1, keepdims=True), sink)
                p = jnp.exp(s - m)
                den = jnp.sum(p, axis=-1, keepdims=True) + jnp.exp(sink - m)
                p_ref[c, rows, :] = p.astype(BF16)
                rl_ref[c, rows, :] = jnp.broadcast_to(1.0 / den, (ch, LANES))
        half_rows = 2 * BLOCK
        pv_ref[c, :half_rows, :] = _dot(p_ref[c, :half_rows, :], vk_ref[c, 0])
        pv_ref[c, half_rows:, :] = _dot(p_ref[c, half_rows:, :], vk_ref[c, 1])

        for i in range(2):
            t = 2 * c + i
            y2 = None
            for blk in (i, 2 + i):
                rows = slice(blk * BLOCK, (blk + 1) * BLOCK)
                pv = pv_ref[c, rows, :]
                rl = rl_ref[c, rows, :]
                ss = jnp.sum(pv * pv, axis=-1, keepdims=True)
                y = pv * (rl * lax.rsqrt(rl * rl * ss / HEAD_DIM + RMS_EPS))
                y2 = y if y2 is None else y2 + y
            o_ref[:, t * LANES:(t + 1) * LANES] = (
                y2 * g_ref[:, t * LANES:(t + 1) * LANES]).astype(o_ref.dtype)


def _swa_attention(qkv, sinks, g_swa):
    b, s, _ = qkv.shape
    nb = s // BLOCK
    qw = 4 * LANES
    kv_spec = lambda blk, prev: pl.BlockSpec(
        (None, BLOCK, LANES),
        (lambda bi, kp, n: (bi, jnp.maximum(n - 1, 0), blk + kp)) if prev
        else (lambda bi, kp, n: (bi, n, blk + kp)))
    return pl.pallas_call(
        _swa_kernel,
        grid=(b, 2, nb),
        in_specs=[pl.BlockSpec(memory_space=pltpu.SMEM),
                  pl.BlockSpec((None, BLOCK, qw), lambda bi, kp, n: (bi, n, kp)),
                  kv_spec(_KA_BLK, True), kv_spec(_KA_BLK, False),
                  kv_spec(_VA_BLK, True), kv_spec(_VA_BLK, False),
                  pl.BlockSpec((1, qw), lambda bi, kp, n: (0, kp))],
        out_specs=pl.BlockSpec((None, BLOCK, qw), lambda bi, kp, n: (bi, n, kp)),
        out_shape=jax.ShapeDtypeStruct((b, s, D_SWA), BF16),
        scratch_shapes=[pltpu.VMEM((2, 4 * BLOCK, LANES), BF16),
                        pltpu.VMEM((2 * BLOCK, LANES), BF16),
                        pltpu.VMEM((2, 2, 2 * BLOCK, LANES), BF16),
                        pltpu.VMEM((2, 4 * BLOCK, 2 * BLOCK), F32),
                        pltpu.VMEM((2, 4 * BLOCK, 2 * BLOCK), BF16),
                        pltpu.VMEM((2, 4 * BLOCK, LANES), F32),
                        pltpu.VMEM((2, 4 * BLOCK, LANES), F32)],
        compiler_params=_params(("parallel", "parallel", "parallel"), 16 << 20),
        name="swa_attention",
    )(sinks, qkv, qkv, qkv, qkv, qkv, g_swa)


_SB_TK = 256
_SB_TQ = 512
_SB_CHUNK = 32
_SB_MASKED = -1e30
_LOG2E = 1.4426950408889634


def _sb_kernel(n_cast, q_ref, k_ref, v_ref, nu_ref, g_ref, *refs):
    w_refs, o_ref, wb_refs = refs[:n_cast], refs[n_cast], refs[n_cast + 1:2 * n_cast + 1]
    qm_ref, z_ref, pb_ref, bt_ref, a_ref, acc_ref, car_ref = refs[2 * n_cast + 1:]
    qb = pl.program_id(2)
    tq, tk, ch = _SB_TQ, _SB_TK, _SB_CHUNK
    ntile = tq // tk
    last = qb * ntile + ntile - 1
    lane = lax.broadcasted_iota(jnp.int32, (tq, LANES), 1)
    q = q_ref[...] * QK_SCALE
    qm_ref[0] = jnp.where(lane < HEAD_DIM, q, jnp.zeros_like(q))
    qm_ref[1] = jnp.where(lane >= HEAD_DIM, q, jnp.zeros_like(q))
    acc_ref[...] = jnp.zeros_like(acc_ref)
    car_ref[...] = jnp.zeros_like(car_ref)
    cmr = (lax.broadcasted_iota(jnp.int32, (ch, tk), 1)
           - lax.broadcasted_iota(jnp.int32, (ch, tk), 0))

    def key_tile(ref, j):
        return ref[pl.ds(pl.multiple_of(j * tk, tk), tk), :]

    def z_stage(slot, j, r_lo):
        k = key_tile(k_ref, j)
        for h in range(2):
            z_ref[slot, h, r_lo:, :] = _dot_nt(qm_ref[h, r_lo:, :], k)

    def p_stage(slot, h, off, r_lo):
        for r0 in range(r_lo, tq, ch):
            rows = slice(r0, r0 + ch)
            t = z_ref[slot, h, rows, :] * _LOG2E
            p = jnp.maximum(t, 0.0) + jnp.log2(1.0 + jnp.exp2(-jnp.abs(t)))
            car = car_ref[h, rows, :]
            tc = t + jnp.concatenate([car, car], axis=1)
            if off is not None:
                causal = cmr < off + r0
                p = jnp.where(causal, p, 0.0)
                tc = jnp.where(causal, tc, _SB_MASKED)
            pb_ref[h, rows, :] = p.astype(BF16)
            z_ref[slot, h, rows, :] = tc
            car_ref[h, rows, :] = car - jnp.sum(p, axis=-1, keepdims=True)

    def sum_stage(h, r_lo):
        bt_ref[h, r_lo:, :] = _dot(pb_ref[h, r_lo:, :], nu_ref[...])

    def w_stage(slot, h, r_lo):
        for r0 in range(r_lo, tq, ch):
            rows = slice(r0, r0 + ch)
            a_ref[h, rows, :] = jnp.exp2(z_ref[slot, h, rows, :] + bt_ref[h, rows, :]).astype(BF16)

    def pv_stage(j, r_lo):
        v = key_tile(v_ref, j)
        for h in range(2):
            acc_ref[h, r_lo:, :] += _dot(a_ref[h, r_lo:, :], v)

    def body(j, slot, off, r_lo=0, r_prev=0, r_next=0):
        if r_prev is not None:
            pv_stage(j + 1, r_prev)
        z_stage(1 - slot, jnp.maximum(j - 1, 0), r_next)
        p_stage(slot, 0, off, r_lo)
        sum_stage(0, r_lo)
        p_stage(slot, 1, off, r_lo)
        sum_stage(1, r_lo)
        w_stage(slot, 0, r_lo)
        w_stage(slot, 1, r_lo)

    assert ntile % 2 == 0
    first_row = [(ntile - 1 - i) * tk for i in range(ntile)] + [0]
    z_stage(0, last, first_row[0])
    for w_ref, wb_ref in zip(w_refs, wb_refs):
        wb_ref[...] = w_ref[...].astype(BF16)
    for i in range(ntile):
        body(last - i, i % 2, (i + 1 - ntile) * tk, first_row[i],
             first_row[i - 1] if i else None, first_row[i + 1])

    @pl.loop(0, qb * (ntile // 2))
    def _(m):
        j = qb * ntile - 1 - 2 * m
        body(j, 0, None)
        body(j - 1, 1, None)

    pv_stage(0, 0)

    ys = []
    for h, half in enumerate([lane < HEAD_DIM, lane >= HEAD_DIM]):
        o = acc_ref[h]
        ms = jnp.sum(jnp.where(half, o * o, 0.0), axis=-1, keepdims=True) / HEAD_DIM
        ys.append(o * lax.rsqrt(ms + RMS_EPS))
    o_ref[...] = (jnp.where(lane < HEAD_DIM, ys[0], ys[1]) * g_ref[...]).astype(o_ref.dtype)


_CAST_ROW_BLOCKS, _CAST_COL_BLOCKS = 32, 8


def _sb_attention(qkv, g_sb, weights):
    b, s, _ = qkv.shape
    npair = D_SB // LANES
    tq, tk = _SB_TQ, _SB_TK
    nq = s // tq
    r = lax.broadcasted_iota(jnp.int32, (tk, tk), 0)
    c = lax.broadcasted_iota(jnp.int32, (tk, tk), 1)
    nu = -(r >= c).astype(BF16)

    assert b * npair * nq == _CAST_ROW_BLOCKS * _CAST_COL_BLOCKS

    def cast_spec(w):
        rows, cols = w.shape
        assert rows % (16 * _CAST_ROW_BLOCKS) == 0 and cols % (LANES * _CAST_COL_BLOCKS) == 0

        def index(bi, p, i):
            step = (bi * npair + p) * nq + i
            return step // _CAST_COL_BLOCKS, step % _CAST_COL_BLOCKS
        return pl.BlockSpec((rows // _CAST_ROW_BLOCKS, cols // _CAST_COL_BLOCKS), index)

    cast_specs = [cast_spec(w) for w in weights]
    outs = pl.pallas_call(
        functools.partial(_sb_kernel, len(weights)),
        grid=(b, npair, nq),
        in_specs=[pl.BlockSpec((None, tq, LANES), lambda bi, p, i: (bi, i, _QB_BLK + p)),
                  pl.BlockSpec((None, s, LANES), lambda bi, p, i: (bi, 0, _KB_BLK + p)),
                  pl.BlockSpec((None, s, LANES), lambda bi, p, i: (bi, 0, _VB_BLK + p)),
                  pl.BlockSpec((tk, tk), lambda bi, p, i: (0, 0)),
                  pl.BlockSpec((1, LANES), lambda bi, p, i: (0, p))] + cast_specs,
        out_specs=[pl.BlockSpec((None, tq, LANES), lambda bi, p, i: (bi, i, p))] + cast_specs,
        out_shape=[jax.ShapeDtypeStruct((b, s, D_SB), BF16)]
        + [jax.ShapeDtypeStruct(w.shape, BF16) for w in weights],
        scratch_shapes=[pltpu.VMEM((2, tq, LANES), BF16),
                        pltpu.VMEM((2, 2, tq, tk), F32),
                        pltpu.VMEM((2, tq, tk), BF16),
                        pltpu.VMEM((2, tq, tk), F32),
                        pltpu.VMEM((2, tq, tk), BF16),
                        pltpu.VMEM((2, tq, LANES), F32),
                        pltpu.VMEM((2, tq, LANES), F32)],
        compiler_params=_params(("parallel", "parallel", "parallel"), 32 << 20),
        name="sb_attention",
    )(qkv, qkv, qkv, nu, g_sb, *weights)
    return outs[0], outs[1:]


_LN_GROUPS = 4


def _proj_ln_q_kernel(oa_ref, ob_ref, wo_ref, x_ref, g_ref, b_ref, wq_ref, h_ref, q_ref,
                      mix_ref, hb_ref):
    bm = x_ref.shape[0]
    groups = [slice(r, r + bm // _LN_GROUPS) for r in range(0, bm, bm // _LN_GROUPS)]
    for rows in groups:
        mix_ref[rows, :] = (_dot(oa_ref[rows, :], wo_ref[:D_SWA, :])
                            + _dot(ob_ref[rows, :], wo_ref[D_SWA:, :]))
    for rows in groups:
        h = _layer_norm(ALPHA * x_ref[rows, :] + mix_ref[rows, :], g_ref[...], b_ref[...])
        h_ref[rows, :] = h
        hb_ref[rows, :] = h.astype(BF16)
    for rows in groups:
        q_ref[rows, :] = _dot(hb_ref[rows, :], wq_ref[...]).astype(BF16)


def _proj_ln_q(o_a, o_b, w_o, x, g, b, w_q, bm):
    m, d = x.shape
    row = lambda w: pl.BlockSpec((bm, w), lambda i: (i, 0))
    full = lambda r, w: pl.BlockSpec((r, w), lambda i: (0, 0), pipeline_mode=pl.Buffered(1))
    vmem = (2 * (2 * bm * D_SWA * 2 + bm * d * (4 + 4 + 2)) + 2 * d * d * 2
            + bm * d * (4 + 2) + 2 * bm * d * 4)
    return pl.pallas_call(
        _proj_ln_q_kernel,
        grid=(m // bm,),
        in_specs=[row(D_SWA), row(D_SB), full(d, d), row(d), full(1, d), full(1, d), full(d, d)],
        out_specs=[row(d), row(d)],
        out_shape=[jax.ShapeDtypeStruct((m, d), F32), jax.ShapeDtypeStruct((m, d), BF16)],
        scratch_shapes=[pltpu.VMEM((bm, d), F32), pltpu.VMEM((bm, d), BF16)],
        compiler_params=_params(("parallel",), vmem),
        name="mix_proj_ln_q",
    )(o_a, o_b, w_o, x, g, b, w_q)


_XATTN_CHUNK = 32


def _xattn_kernel(q_ref, kv_ref, w_ref, h_ref, g_ref, b_ref, o_ref, ob_ref,
                  s_ref, p_ref, rl_ref, c_ref, y_ref):
    bm = q_ref.shape[0]
    ch = _XATTN_CHUNK
    scale = 1.0 / math.sqrt(HEAD_DIM_MEM)
    heads = [slice(hd * HEAD_DIM_MEM, (hd + 1) * HEAD_DIM_MEM) for hd in range(N_HEADS_MEM)]
    for hd, cols in enumerate(heads):
        s_ref[hd] = _dot_nt(q_ref[:, cols], kv_ref[:, cols])
    for hd, cols in enumerate(heads):
        for r0 in range(0, bm, ch):
            rows = slice(r0, r0 + ch)
            s = s_ref[hd, rows, :] * scale
            p = jnp.exp(s - jnp.max(s, axis=-1, keepdims=True))
            l = jnp.sum(p, axis=-1, keepdims=True)
            p_ref[hd, rows, :] = p.astype(BF16)
            rl_ref[hd, rows, :] = jnp.broadcast_to(1.0 / l, (ch, LANES))
        o = _dot(p_ref[hd], kv_ref[:, D_MODEL + cols.start:D_MODEL + cols.stop])
        rl = rl_ref[hd]
        c_ref[:, cols] = (o * jnp.concatenate([rl] * (HEAD_DIM_MEM // LANES), axis=1)).astype(BF16)
    groups = [slice(r, r + bm // _LN_GROUPS) for r in range(0, bm, bm // _LN_GROUPS)]
    for rows in groups:
        y_ref[rows, :] = _dot(c_ref[rows, :], w_ref[...])
    for rows in groups:
        h = _layer_norm(ALPHA * h_ref[rows, :] + y_ref[rows, :], g_ref[...], b_ref[...])
        o_ref[rows, :] = h
        ob_ref[rows, :] = h.astype(BF16)


def _xattn_ln(q, kv, w_o, h, g, b, bm):
    bsz, s, d = q.shape
    mlen = kv.shape[1]
    row = lambda: pl.BlockSpec((None, bm, d), lambda bi, i: (bi, i, 0))
    full = lambda r, w: pl.BlockSpec((r, w), lambda bi, i: (0, 0), pipeline_mode=pl.Buffered(1))
    vmem = (2 * (bm * d * 2 + mlen * 2 * d * 2 + bm * d * (4 + 4 + 2)) + d * d * 2
            + N_HEADS_MEM * bm * (mlen * 6 + LANES * 4) + bm * d * (2 + 4) + 2 * bm * d * 4)
    return pl.pallas_call(
        _xattn_kernel,
        grid=(bsz, s // bm),
        in_specs=[row(), pl.BlockSpec((None, mlen, 2 * d), lambda bi, i: (bi, 0, 0)),
                  full(d, d), row(), full(1, d), full(1, d)],
        out_specs=[row(), row()],
        out_shape=[jax.ShapeDtypeStruct((bsz, s, d), F32),
                   jax.ShapeDtypeStruct((bsz, s, d), BF16)],
        scratch_shapes=[pltpu.VMEM((N_HEADS_MEM, bm, mlen), F32),
                        pltpu.VMEM((N_HEADS_MEM, bm, mlen), BF16),
                        pltpu.VMEM((N_HEADS_MEM, bm, LANES), F32),
                        pltpu.VMEM((bm, d), BF16),
                        pltpu.VMEM((bm, d), F32)],
        compiler_params=_params(("parallel", "parallel"), vmem),
        name="mem_xattn_ln",
    )(q, kv, w_o, h, g, b)


def _ffn_kernel(xb_ref, wg_ref, wu_ref, wd_ref, h_ref, g_ref, b_ref, o_ref, acc_ref):
    j = pl.program_id(1)

    @pl.when(j == 0)
    def _():
        acc_ref[...] = jnp.zeros_like(acc_ref)

    x = xb_ref[...]
    gate = _dot(x, wg_ref[...])
    up = _dot(x, wu_ref[...])
    act = gate / (1.0 + jnp.exp(-gate)) * up
    acc_ref[...] += _dot(act.astype(BF16), wd_ref[...])

    @pl.when(j == pl.num_programs(1) - 1)
    def _():
        o_ref[...] = _layer_norm(ALPHA * h_ref[...] + acc_ref[...], g_ref[...], b_ref[...])


def _ffn_ln(hb, w_gu, w_d, h, g, b, bm, bf):
    m, d = h.shape
    dff = w_d.shape[0]
    nf = dff // bf
    assert dff % bf == 0 and m % bm == 0
    row = lambda: pl.BlockSpec((bm, d), lambda i, j: (i, 0))
    full = lambda: pl.BlockSpec((1, d), lambda i, j: (0, 0))
    vmem = 2 * (bm * d * 2 + 3 * d * bf * 2 + 2 * bm * d * 4) + bm * d * 4 + 4 * bm * bf * 4
    return pl.pallas_call(
        _ffn_kernel,
        grid=(m // bm, nf),
        in_specs=[row(),
                  pl.BlockSpec((d, bf), lambda i, j: (0, j)),
                  pl.BlockSpec((d, bf), lambda i, j: (0, j + nf)),
                  pl.BlockSpec((bf, d), lambda i, j: (j, 0)),
                  row(), full(), full()],
        out_specs=row(),
        out_shape=jax.ShapeDtypeStruct((m, d), F32),
        scratch_shapes=[pltpu.VMEM((bm, d), F32)],
        compiler_params=_params(("parallel", "arbitrary"), vmem + (4 << 20)),
        name="swiglu_ffn_ln",
    )(hb, w_gu, w_gu, w_d, h, g, b)


def kernel(x, mem, w_in, sinks, g_swa, g_sb, w_o, ln1_g, ln1_b, w_q_mem, w_kv_mem,
           w_o_mem, ln2_g, ln2_b, w_gate_up, w_down, ln3_g, ln3_b):
    bsz, s, d = x.shape
    t = bsz * s
    mlen = mem.shape[1]
    h = x.reshape(t, d)
    for l in range(DEPTH):
        row = lambda a: a[l].reshape(1, -1)
        qkv = _matmul(h, w_in[l].astype(BF16), 1024, 1536, BF16, "qkv_proj")
        qkv = qkv.reshape(bsz, s, D_IN)
        o_a = _swa_attention(qkv, sinks[l], row(g_swa))
        o_b, (wo_b, wq_b, wkv_b, wom_b, wgu_b, wd_b) = _sb_attention(
            qkv, row(g_sb),
            [w_o[l], w_q_mem[l], w_kv_mem[l], w_o_mem[l], w_gate_up[l], w_down[l]])
        h, q = _proj_ln_q(o_a.reshape(t, D_SWA), o_b.reshape(t, D_SB), wo_b,
                          h, row(ln1_g), row(ln1_b), wq_b, 512)
        kv = _matmul(mem.reshape(bsz * mlen, d), wkv_b, 1024, 1024, BF16, "mem_kv_proj")
        h3, hb3 = _xattn_ln(q.reshape(bsz, s, d), kv.reshape(bsz, mlen, 2 * d),
                            wom_b, h.reshape(bsz, s, d), row(ln2_g), row(ln2_b), 512)
        h, hb = h3.reshape(t, d), hb3.reshape(t, d)
        h = _ffn_ln(hb, wgu_b, wd_b, h, row(ln3_g), row(ln3_b), 512, 512)
    return h.reshape(bsz, s, d)
```

```python
import functools
import math

import jax
import jax.numpy as jnp
from jax import lax
from jax.experimental import pallas as pl
from jax.experimental.pallas import tpu as pltpu

F32 = jnp.float32
BF16 = jnp.bfloat16

D_MODEL = 2048
HEAD_DIM = 64
N_HEADS_SWA = 16
N_KV_SWA = 4
N_HEADS_SB = 16
WINDOW = 128
BLOCK = 128
N_HEADS_MEM = 4
HEAD_DIM_MEM = D_MODEL // N_HEADS_MEM
D_SWA = N_HEADS_SWA * HEAD_DIM
D_KV_SWA = N_KV_SWA * HEAD_DIM
D_SB = N_HEADS_SB * HEAD_DIM
D_IN = D_SWA + 2 * D_KV_SWA + 3 * D_SB
DEPTH = 1
ALPHA = (2.0 * DEPTH) ** 0.25
LN_EPS = 1e-5
RMS_EPS = 1e-6
QK_SCALE = 1.0 / math.sqrt(HEAD_DIM)

LANES = 128
V7X_VMEM_LIMIT = 56 * 1024 * 1024

_KA_BLK = D_SWA // LANES
_VA_BLK = (D_SWA + D_KV_SWA) // LANES
_QB_BLK = (D_SWA + 2 * D_KV_SWA) // LANES
_KB_BLK = _QB_BLK + D_SB // LANES
_VB_BLK = _KB_BLK + D_SB // LANES


def _params(sem, vmem_bytes, flags=None):
    return pltpu.CompilerParams(
        dimension_semantics=sem, vmem_limit_bytes=min(int(vmem_bytes), V7X_VMEM_LIMIT),
        flags=flags)


def _dot(a, b):
    return jnp.dot(a, b, preferred_element_type=F32)


def _dot_nt(a, b):
    return lax.dot_general(a, b, (((1,), (1,)), ((), ())), preferred_element_type=F32)


def _layer_norm(y, g, b):
    mu = jnp.mean(y, axis=-1, keepdims=True)
    d = y - mu
    var = jnp.mean(d * d, axis=-1, keepdims=True)
    return d * lax.rsqrt(var + LN_EPS) * g + b


def _mm_kernel(x_ref, w_ref, o_ref, *xb_ref):
    if xb_ref:
        @pl.when(pl.program_id(1) == 0)
        def _():
            xb_ref[0][...] = x_ref[...].astype(BF16)
        x = xb_ref[0][...]
    else:
        x = x_ref[...]
    o_ref[...] = _dot(x, w_ref[...]).astype(o_ref.dtype)


def _matmul(x, w, bm, bn, out_dtype, name):
    m, k = x.shape
    n = w.shape[1]
    assert m % bm == 0 and n % bn == 0
    osz = jnp.dtype(out_dtype).itemsize
    xsz = jnp.dtype(x.dtype).itemsize
    cast = x.dtype != BF16
    vmem = 2 * (bm * k * xsz + k * bn * 2 + bm * bn * osz) + bm * bn * 4 + cast * bm * k * 2
    return pl.pallas_call(
        _mm_kernel,
        grid=(m // bm, n // bn),
        in_specs=[pl.BlockSpec((bm, k), lambda i, j: (i, 0)),
                  pl.BlockSpec((k, bn), lambda i, j: (0, j))],
        out_specs=pl.BlockSpec((bm, bn), lambda i, j: (i, j)),
        out_shape=jax.ShapeDtypeStruct((m, n), out_dtype),
        scratch_shapes=[pltpu.VMEM((bm, k), BF16)] if cast else [],
        compiler_params=_params(("parallel", "arbitrary"), vmem + (8 << 20)),
        name=name,
    )(x, w)


def _alibi_slope(h):
    return 2.0 ** (-8.0 * (h + 1) / N_HEADS_SWA)


_SWA_CHUNK = 32


def _swa_kernel(sink_ref, q_ref, kp_ref, kc_ref, vp_ref, vc_ref, g_ref, o_ref,
                qs_ref, k_s, vk_ref, s_ref, p_ref, rl_ref, pv_ref):
    kvp = pl.program_id(1)
    n = pl.program_id(2)
    ch = _SWA_CHUNK
    k_s[:BLOCK] = kp_ref[...]
    k_s[BLOCK:] = kc_ref[...]
    lane = lax.broadcasted_iota(jnp.int32, (BLOCK, LANES), 1)
    halves = [lane < HEAD_DIM, lane >= HEAD_DIM]

    def stack_heads(c):
        return [(2 * c, c), (2 * c + 1, c), (2 * c, 1 - c), (2 * c + 1, 1 - c)]

    for t in range(4):
        c = t // 2
        q_t = q_ref[:, t * LANES:(t + 1) * LANES]
        q_sw = pltpu.roll(q_t.astype(F32), HEAD_DIM, 1).astype(BF16)
        for blk, (tt, u) in enumerate(stack_heads(c)):
            if tt == t:
                src = q_t if u == c else q_sw
                qs_ref[c, blk * BLOCK:(blk + 1) * BLOCK, :] = (
                    jnp.where(halves[c], src, jnp.zeros_like(src)) * QK_SCALE)

    for c in range(2):
        s_ref[c] = _dot_nt(qs_ref[c], k_s[...])

    v = jnp.concatenate([vp_ref[...], vc_ref[...]], axis=0).astype(F32)
    v_sw = pltpu.roll(v, HEAD_DIM, 1)
    lane2 = lax.broadcasted_iota(jnp.int32, (2 * BLOCK, LANES), 1)
    for c in range(2):
        in_c = (lane2 >= HEAD_DIM) if c else (lane2 < HEAD_DIM)
        vk_ref[c, 0] = jnp.where(in_c, v, 0.0).astype(BF16)
        vk_ref[c, 1] = jnp.where(in_c, 0.0, v_sw).astype(BF16)

    qi = lax.broadcasted_iota(jnp.int32, (ch, 2 * BLOCK), 0)
    kj = lax.broadcasted_iota(jnp.int32, (ch, 2 * BLOCK), 1)
    pens = []
    for r0 in range(0, BLOCK, ch):
        dist = qi + (r0 + BLOCK) - kj
        valid = (dist >= 0) & (dist < WINDOW) & ((kj >= BLOCK) | (n > 0))
        pens.append(jnp.where(valid, dist.astype(F32), jnp.inf))

    for c in range(2):
        for blk, (t, u) in enumerate(stack_heads(c)):
            slope = jnp.where(kvp == 0, _alibi_slope(t * 2 + u), _alibi_slope(8 + t * 2 + u))
            sink = sink_ref[kvp * 8 + t * 2 + u]
            for i, pen in enumerate(pens):
                r0 = blk * BLOCK + i * ch
                rows = slice(r0, r0 + ch)
                s = s_ref[c, rows, :] - slope * pen
                m = jnp.maximum(jnp.max(s, axis=-1, keepdims=True), sink)
                p = jnp.exp(s - m)
                den = jnp.sum(p, axis=-1, keepdims=True) + jnp.exp(sink - m)
                p_ref[c, rows, :] = p.astype(BF16)
                rl_ref[c, rows, :] = jnp.broadcast_to(1.0 / den, (ch, LANES))
        half_rows = 2 * BLOCK
        pv_ref[c, :half_rows, :] = _dot(p_ref[c, :half_rows, :], vk_ref[c, 0])
        pv_ref[c, half_rows:, :] = _dot(p_ref[c, half_rows:, :], vk_ref[c, 1])

        for i in range(2):
            t = 2 * c + i
            y2 = None
            for blk in (i, 2 + i):
                rows = slice(blk * BLOCK, (blk + 1) * BLOCK)
                pv = pv_ref[c, rows, :]
                rl = rl_ref[c, rows, :]
                ss = jnp.sum(pv * pv, axis=-1, keepdims=True)
                y = pv * (rl * lax.rsqrt(rl * rl * ss / HEAD_DIM + RMS_EPS))
                y2 = y if y2 is None else y2 + y
            o_ref[:, t * LANES:(t + 1) * LANES] = (
                y2 * g_ref[:, t * LANES:(t + 1) * LANES]).astype(o_ref.dtype)


def _swa_attention(qkv, sinks, g_swa):
    b, s, _ = qkv.shape
    nb = s // BLOCK
    qw = 4 * LANES
    kv_spec = lambda blk, prev: pl.BlockSpec(
        (None, BLOCK, LANES),
        (lambda bi, kp, n: (bi, jnp.maximum(n - 1, 0), blk + kp)) if prev
        else (lambda bi, kp, n: (bi, n, blk + kp)))
    return pl.pallas_call(
        _swa_kernel,
        grid=(b, 2, nb),
        in_specs=[pl.BlockSpec(memory_space=pltpu.SMEM),
                  pl.BlockSpec((None, BLOCK, qw), lambda bi, kp, n: (bi, n, kp)),
                  kv_spec(_KA_BLK, True), kv_spec(_KA_BLK, False),
                  kv_spec(_VA_BLK, True), kv_spec(_VA_BLK, False),
                  pl.BlockSpec((1, qw), lambda bi, kp, n: (0, kp))],
        out_specs=pl.BlockSpec((None, BLOCK, qw), lambda bi, kp, n: (bi, n, kp)),
        out_shape=jax.ShapeDtypeStruct((b, s, D_SWA), BF16),
        scratch_shapes=[pltpu.VMEM((2, 4 * BLOCK, LANES), BF16),
                        pltpu.VMEM((2 * BLOCK, LANES), BF16),
                        pltpu.VMEM((2, 2, 2 * BLOCK, LANES), BF16),
                        pltpu.VMEM((2, 4 * BLOCK, 2 * BLOCK), F32),
                        pltpu.VMEM((2, 4 * BLOCK, 2 * BLOCK), BF16),
                        pltpu.VMEM((2, 4 * BLOCK, LANES), F32),
                        pltpu.VMEM((2, 4 * BLOCK, LANES), F32)],
        compiler_params=_params(("parallel", "parallel", "parallel"), 16 << 20),
        name="swa_attention",
    )(sinks, qkv, qkv, qkv, qkv, qkv, g_swa)


_SB_TK = 256
_SB_TQ = 512
_SB_CHUNK = 32
_SB_MASKED = -1e30
_LOG2E = 1.4426950408889634


def _sb_kernel(n_cast, q_ref, k_ref, v_ref, nu_ref, g_ref, *refs):
    w_refs, o_ref, wb_refs = refs[:n_cast], refs[n_cast], refs[n_cast + 1:2 * n_cast + 1]
    qm_ref, z_ref, pb_ref, bt_ref, a_ref, acc_ref, car_ref = refs[2 * n_cast + 1:]
    for w_ref, wb_ref in zip(w_refs, wb_refs):
        wb_ref[...] = w_ref[...].astype(BF16)

    tq, tk, ch = _SB_TQ, _SB_TK, _SB_CHUNK
    ntile = tq // tk
    lane = lax.broadcasted_iota(jnp.int32, (tq, LANES), 1)
    cmr = (lax.broadcasted_iota(jnp.int32, (ch, tk), 1)
           - lax.broadcasted_iota(jnp.int32, (ch, tk), 0))

    def key_tile(ref, j):
        return ref[pl.ds(pl.multiple_of(j * tk, tk), tk), :]

    def z_stage(slot, j, r_lo):
        k = key_tile(k_ref, j)
        for h in range(2):
            z_ref[slot, h, r_lo:, :] = _dot_nt(qm_ref[h, r_lo:, :], k)

    def p_stage(slot, h, off, r_lo):
        for r0 in range(r_lo, tq, ch):
            rows = slice(r0, r0 + ch)
            t = z_ref[slot, h, rows, :] * _LOG2E
            p = jnp.maximum(t, 0.0) + jnp.log2(1.0 + jnp.exp2(-jnp.abs(t)))
            car = car_ref[h, rows, :]
            tc = t + jnp.concatenate([car, car], axis=1)
            if off is not None:
                causal = cmr < off + r0
                p = jnp.where(causal, p, 0.0)
                tc = jnp.where(causal, tc, _SB_MASKED)
            pb_ref[h, rows, :] = p.astype(BF16)
            z_ref[slot, h, rows, :] = tc
            car_ref[h, rows, :] = car - jnp.sum(p, axis=-1, keepdims=True)

    def sum_stage(h, r_lo):
        bt_ref[h, r_lo:, :] = _dot(pb_ref[h, r_lo:, :], nu_ref[...])

    def w_stage(slot, h, r_lo):
        for r0 in range(r_lo, tq, ch):
            rows = slice(r0, r0 + ch)
            a_ref[h, rows, :] = jnp.exp2(z_ref[slot, h, rows, :] + bt_ref[h, rows, :]).astype(BF16)

    def pv_stage(j, r_lo):
        v = key_tile(v_ref, j)
        for h in range(2):
            acc_ref[h, r_lo:, :] += _dot(a_ref[h, r_lo:, :], v)

    def body(j, slot, off, r_lo=0, r_prev=0, r_next=0):
        if r_prev is not None:
            pv_stage(j + 1, r_prev)
        z_stage(1 - slot, jnp.maximum(j - 1, 0), r_next)
        p_stage(slot, 0, off, r_lo)
        sum_stage(0, r_lo)
        p_stage(slot, 1, off, r_lo)
        sum_stage(1, r_lo)
        w_stage(slot, 0, r_lo)
        w_stage(slot, 1, r_lo)

    assert ntile % 2 == 0
    first_row = [(ntile - 1 - i) * tk for i in range(ntile)] + [0]

    @pl.loop(0, q_ref.shape[0] // tq)
    def _(qb):
        q_rows = pl.ds(pl.multiple_of(qb * tq, tq), tq)
        last = qb * ntile + ntile - 1
        q = q_ref[q_rows, :] * QK_SCALE
        qm_ref[0] = jnp.where(lane < HEAD_DIM, q, jnp.zeros_like(q))
        qm_ref[1] = jnp.where(lane >= HEAD_DIM, q, jnp.zeros_like(q))
        acc_ref[...] = jnp.zeros_like(acc_ref)
        car_ref[...] = jnp.zeros_like(car_ref)

        z_stage(0, last, first_row[0])
        for i in range(ntile):
            body(last - i, i % 2, (i + 1 - ntile) * tk, first_row[i],
                 first_row[i - 1] if i else None, first_row[i + 1])

        @pl.loop(0, qb * (ntile // 2))
        def _(m):
            j = qb * ntile - 1 - 2 * m
            body(j, 0, None)
            body(j - 1, 1, None)

        pv_stage(0, 0)

        ys = []
        for h, half in enumerate([lane < HEAD_DIM, lane >= HEAD_DIM]):
            o = acc_ref[h]
            ms = jnp.sum(jnp.where(half, o * o, 0.0), axis=-1, keepdims=True) / HEAD_DIM
            ys.append(o * lax.rsqrt(ms + RMS_EPS))
        o_ref[q_rows, :] = (jnp.where(lane < HEAD_DIM, ys[0], ys[1]) * g_ref[...]).astype(o_ref.dtype)


_CAST_ROW_BLOCKS, _CAST_COL_BLOCKS = 8, 8


def _sb_attention(qkv, g_sb, weights):
    b, s, _ = qkv.shape
    npair = D_SB // LANES
    tq, tk = _SB_TQ, _SB_TK
    r = lax.broadcasted_iota(jnp.int32, (tk, tk), 0)
    c = lax.broadcasted_iota(jnp.int32, (tk, tk), 1)
    nu = -(r >= c).astype(BF16)

    assert b * npair == _CAST_ROW_BLOCKS * _CAST_COL_BLOCKS

    def cast_spec(w):
        rows, cols = w.shape
        assert rows % (16 * _CAST_ROW_BLOCKS) == 0 and cols % (LANES * _CAST_COL_BLOCKS) == 0

        def index(bi, p):
            step = bi * npair + p
            return (lax.shift_right_logical(step, _CAST_COL_BLOCKS.bit_length() - 1),
                    jnp.bitwise_and(step, _CAST_COL_BLOCKS - 1))
        return pl.BlockSpec((rows // _CAST_ROW_BLOCKS, cols // _CAST_COL_BLOCKS), index)

    cast_specs = [cast_spec(w) for w in weights]
    outs = pl.pallas_call(
        functools.partial(_sb_kernel, len(weights)),
        grid=(b, npair),
        in_specs=[pl.BlockSpec((None, s, LANES), lambda bi, p: (bi, 0, _QB_BLK + p)),
                  pl.BlockSpec((None, s, LANES), lambda bi, p: (bi, 0, _KB_BLK + p)),
                  pl.BlockSpec((None, s, LANES), lambda bi, p: (bi, 0, _VB_BLK + p)),
                  pl.BlockSpec((tk, tk), lambda bi, p: (0, 0)),
                  pl.BlockSpec((1, LANES), lambda bi, p: (0, p))] + cast_specs,
        out_specs=[pl.BlockSpec((None, s, LANES), lambda bi, p: (bi, 0, p))] + cast_specs,
        out_shape=[jax.ShapeDtypeStruct((b, s, D_SB), BF16)]
        + [jax.ShapeDtypeStruct(w.shape, BF16) for w in weights],
        scratch_shapes=[pltpu.VMEM((2, tq, LANES), BF16),
                        pltpu.VMEM((2, 2, tq, tk), F32),
                        pltpu.VMEM((2, tq, tk), BF16),
                        pltpu.VMEM((2, tq, tk), F32),
                        pltpu.VMEM((2, tq, tk), BF16),
                        pltpu.VMEM((2, tq, LANES), F32),
                        pltpu.VMEM((2, tq, LANES), F32)],
        compiler_params=_params(("parallel", "parallel"), 32 << 20),
        name="sb_attention",
    )(qkv, qkv, qkv, nu, g_sb, *weights)
    return outs[0], outs[1:]


_LN_GROUPS = 4


def _proj_ln_q_kernel(oa_ref, ob_ref, wo_ref, x_ref, g_ref, b_ref, wq_ref, h_ref, q_ref,
                      mix_ref, hb_ref):
    bm = x_ref.shape[0]
    groups = [slice(r, r + bm // _LN_GROUPS) for r in range(0, bm, bm // _LN_GROUPS)]
    for rows in groups:
        mix_ref[rows, :] = (_dot(oa_ref[rows, :], wo_ref[:D_SWA, :])
                            + _dot(ob_ref[rows, :], wo_ref[D_SWA:, :]))
    for rows in groups:
        h = _layer_norm(ALPHA * x_ref[rows, :] + mix_ref[rows, :], g_ref[...], b_ref[...])
        h_ref[rows, :] = h
        hb_ref[rows, :] = h.astype(BF16)
    for rows in groups:
        q_ref[rows, :] = _dot(hb_ref[rows, :], wq_ref[...]).astype(BF16)


def _proj_ln_q(o_a, o_b, w_o, x, g, b, w_q, bm):
    m, d = x.shape
    row = lambda w: pl.BlockSpec((bm, w), lambda i: (i, 0))
    full = lambda r, w: pl.BlockSpec((r, w), lambda i: (0, 0), pipeline_mode=pl.Buffered(1))
    vmem = (2 * (2 * bm * D_SWA * 2 + bm * d * (4 + 4 + 2)) + 2 * d * d * 2
            + bm * d * (4 + 2) + 2 * bm * d * 4)
    return pl.pallas_call(
        _proj_ln_q_kernel,
        grid=(m // bm,),
        in_specs=[row(D_SWA), row(D_SB), full(d, d), row(d), full(1, d), full(1, d), full(d, d)],
        out_specs=[row(d), row(d)],
        out_shape=[jax.ShapeDtypeStruct((m, d), F32), jax.ShapeDtypeStruct((m, d), BF16)],
        scratch_shapes=[pltpu.VMEM((bm, d), F32), pltpu.VMEM((bm, d), BF16)],
        compiler_params=_params(("parallel",), vmem),
        name="mix_proj_ln_q",
    )(o_a, o_b, w_o, x, g, b, w_q)


_XATTN_CHUNK = 32


def _xattn_kernel(q_ref, kv_ref, w_ref, h_ref, g_ref, b_ref, o_ref, ob_ref,
                  s_ref, p_ref, rl_ref, c_ref, y_ref):
    bm = q_ref.shape[0]
    ch = _XATTN_CHUNK
    scale = 1.0 / math.sqrt(HEAD_DIM_MEM)
    heads = [slice(hd * HEAD_DIM_MEM, (hd + 1) * HEAD_DIM_MEM) for hd in range(N_HEADS_MEM)]
    for hd, cols in enumerate(heads):
        s_ref[hd] = _dot_nt(q_ref[:, cols], kv_ref[:, cols])
    for hd, cols in enumerate(heads):
        for r0 in range(0, bm, ch):
            rows = slice(r0, r0 + ch)
            s = s_ref[hd, rows, :] * scale
            p = jnp.exp(s - jnp.max(s, axis=-1, keepdims=True))
            l = jnp.sum(p, axis=-1, keepdims=True)
            p_ref[hd, rows, :] = p.astype(BF16)
            rl_ref[hd, rows, :] = jnp.broadcast_to(1.0 / l, (ch, LANES))
        o = _dot(p_ref[hd], kv_ref[:, D_MODEL + cols.start:D_MODEL + cols.stop])
        rl = rl_ref[hd]
        c_ref[:, cols] = (o * jnp.concatenate([rl] * (HEAD_DIM_MEM // LANES), axis=1)).astype(BF16)
    groups = [slice(r, r + bm // _LN_GROUPS) for r in range(0, bm, bm // _LN_GROUPS)]
    for rows in groups:
        y_ref[rows, :] = _dot(c_ref[rows, :], w_ref[...])
    for rows in groups:
        h = _layer_norm(ALPHA * h_ref[rows, :] + y_ref[rows, :], g_ref[...], b_ref[...])
        o_ref[rows, :] = h
        ob_ref[rows, :] = h.astype(BF16)


def _xattn_ln(q, kv, w_o, h, g, b, bm):
    bsz, s, d = q.shape
    mlen = kv.shape[1]
    row = lambda: pl.BlockSpec((None, bm, d), lambda bi, i: (bi, i, 0))
    full = lambda r, w: pl.BlockSpec((r, w), lambda bi, i: (0, 0), pipeline_mode=pl.Buffered(1))
    vmem = (2 * (bm * d * 2 + mlen * 2 * d * 2 + bm * d * (4 + 4 + 2)) + d * d * 2
            + N_HEADS_MEM * bm * (mlen * 6 + LANES * 4) + bm * d * (2 + 4) + 2 * bm * d * 4)
    return pl.pallas_call(
        _xattn_kernel,
        grid=(bsz, s // bm),
        in_specs=[row(), pl.BlockSpec((None, mlen, 2 * d), lambda bi, i: (bi, 0, 0)),
                  full(d, d), row(), full(1, d), full(1, d)],
        out_specs=[row(), row()],
        out_shape=[jax.ShapeDtypeStruct((bsz, s, d), F32),
                   jax.ShapeDtypeStruct((bsz, s, d), BF16)],
        scratch_shapes=[pltpu.VMEM((N_HEADS_MEM, bm, mlen), F32),
                        pltpu.VMEM((N_HEADS_MEM, bm, mlen), BF16),
                        pltpu.VMEM((N_HEADS_MEM, bm, LANES), F32),
                        pltpu.VMEM((bm, d), BF16),
                        pltpu.VMEM((bm, d), F32)],
        compiler_params=_params(("parallel", "parallel"), vmem),
        name="mem_xattn_ln",
    )(q, kv, w_o, h, g, b)


def _ffn_kernel(xb_ref, wg_ref, wu_ref, wd_ref, h_ref, g_ref, b_ref, o_ref, acc_ref):
    j = pl.program_id(1)

    @pl.when(j == 0)
    def _():
        acc_ref[...] = jnp.zeros_like(acc_ref)

    x = xb_ref[...]
    gate = _dot(x, wg_ref[...])
    up = _dot(x, wu_ref[...])
    act = gate / (1.0 + jnp.exp(-gate)) * up
    acc_ref[...] += _dot(act.astype(BF16), wd_ref[...])

    @pl.when(j == pl.num_programs(1) - 1)
    def _():
        o_ref[...] = _layer_norm(ALPHA * h_ref[...] + acc_ref[...], g_ref[...], b_ref[...])


def _ffn_ln(hb, w_gu, w_d, h, g, b, bm, bf):
    m, d = h.shape
    dff = w_d.shape[0]
    nf = dff // bf
    assert dff % bf == 0 and m % bm == 0
    row = lambda: pl.BlockSpec((bm, d), lambda i, j: (i, 0))
    full = lambda: pl.BlockSpec((1, d), lambda i, j: (0, 0))
    vmem = 2 * (bm * d * 2 + 3 * d * bf * 2 + 2 * bm * d * 4) + bm * d * 4 + 4 * bm * bf * 4
    return pl.pallas_call(
        _ffn_kernel,
        grid=(m // bm, nf),
        in_specs=[row(),
                  pl.BlockSpec((d, bf), lambda i, j: (0, j)),
                  pl.BlockSpec((d, bf), lambda i, j: (0, j + nf)),
                  pl.BlockSpec((bf, d), lambda i, j: (j, 0)),
                  row(), full(), full()],
        out_specs=row(),
        out_shape=jax.ShapeDtypeStruct((m, d), F32),
        scratch_shapes=[pltpu.VMEM((bm, d), F32)],
        compiler_params=_params(("parallel", "arbitrary"), vmem + (4 << 20)),
        name="swiglu_ffn_ln",
    )(hb, w_gu, w_gu, w_d, h, g, b)


def kernel(x, mem, w_in, sinks, g_swa, g_sb, w_o, ln1_g, ln1_b, w_q_mem, w_kv_mem,
           w_o_mem, ln2_g, ln2_b, w_gate_up, w_down, ln3_g, ln3_b):
    bsz, s, d = x.shape
    t = bsz * s
    mlen = mem.shape[1]
    h = x.reshape(t, d)
    for l in range(DEPTH):
        row = lambda a: a[l].reshape(1, -1)
        qkv = _matmul(h, w_in[l].astype(BF16), 1024, 1536, BF16, "qkv_proj")
        qkv = qkv.reshape(bsz, s, D_IN)
        o_a = _swa_attention(qkv, sinks[l], row(g_swa))
        o_b, (wo_b, wq_b, wkv_b, wom_b, wgu_b, wd_b) = _sb_attention(
            qkv, row(g_sb),
            [w_o[l], w_q_mem[l], w_kv_mem[l], w_o_mem[l], w_gate_up[l], w_down[l]])
        h, q = _proj_ln_q(o_a.reshape(t, D_SWA), o_b.reshape(t, D_SB), wo_b,
                          h, row(ln1_g), row(ln1_b), wq_b, 512)
        kv = _matmul(mem.reshape(bsz * mlen, d), wkv_b, 1024, 1024, BF16, "mem_kv_proj")
        h3, hb3 = _xattn_ln(q.reshape(bsz, s, d), kv.reshape(bsz, mlen, 2 * d),
                            wom_b, h.reshape(bsz, s, d), row(ln2_g), row(ln2_b), 512)
        h, hb = h3.reshape(t, d), hb3.reshape(t, d)
        h = _ffn_ln(hb, wgu_b, wd_b, h, row(ln3_g), row(ln3_b), 512, 512)
    return h.reshape(bsz, s, d)
```

```python
import functools
import math

import jax
import jax.numpy as jnp
from jax import lax
from jax.experimental import pallas as pl
from jax.experimental.pallas import tpu as pltpu

F32 = jnp.float32
BF16 = jnp.bfloat16

D_MODEL = 2048
HEAD_DIM = 64
N_HEADS_SWA = 16
N_KV_SWA = 4
N_HEADS_SB = 16
WINDOW = 128
BLOCK = 128
N_HEADS_MEM = 4
HEAD_DIM_MEM = D_MODEL // N_HEADS_MEM
D_SWA = N_HEADS_SWA * HEAD_DIM
D_KV_SWA = N_KV_SWA * HEAD_DIM
D_SB = N_HEADS_SB * HEAD_DIM
D_IN = D_SWA + 2 * D_KV_SWA + 3 * D_SB
DEPTH = 1
ALPHA = (2.0 * DEPTH) ** 0.25
LN_EPS = 1e-5
RMS_EPS = 1e-6
QK_SCALE = 1.0 / math.sqrt(HEAD_DIM)

LANES = 128
V7X_VMEM_LIMIT = 56 * 1024 * 1024

_KA_BLK = D_SWA // LANES
_VA_BLK = (D_SWA + D_KV_SWA) // LANES
_QB_BLK = (D_SWA + 2 * D_KV_SWA) // LANES
_KB_BLK = _QB_BLK + D_SB // LANES
_VB_BLK = _KB_BLK + D_SB // LANES


def _params(sem, vmem_bytes, flags=None):
    return pltpu.CompilerParams(
        dimension_semantics=sem, vmem_limit_bytes=min(int(vmem_bytes), V7X_VMEM_LIMIT),
        flags=flags)


def _dot(a, b):
    return jnp.dot(a, b, preferred_element_type=F32)


def _dot_nt(a, b):
    return lax.dot_general(a, b, (((1,), (1,)), ((), ())), preferred_element_type=F32)


def _layer_norm(y, g, b):
    mu = jnp.mean(y, axis=-1, keepdims=True)
    d = y - mu
    var = jnp.mean(d * d, axis=-1, keepdims=True)
    return d * lax.rsqrt(var + LN_EPS) * g + b


def _mm_kernel(x_ref, w_ref, o_ref, *xb_ref):
    if xb_ref:
        @pl.when(pl.program_id(1) == 0)
        def _():
            xb_ref[0][...] = x_ref[...].astype(BF16)
        x = xb_ref[0][...]
    else:
        x = x_ref[...]
    o_ref[...] = _dot(x, w_ref[...]).astype(o_ref.dtype)


def _matmul(x, w, bm, bn, out_dtype, name):
    m, k = x.shape
    n = w.shape[1]
    assert m % bm == 0 and n % bn == 0
    osz = jnp.dtype(out_dtype).itemsize
    xsz = jnp.dtype(x.dtype).itemsize
    cast = x.dtype != BF16
    vmem = 2 * (bm * k * xsz + k * bn * 2 + bm * bn * osz) + bm * bn * 4 + cast * bm * k * 2
    return pl.pallas_call(
        _mm_kernel,
        grid=(m // bm, n // bn),
        in_specs=[pl.BlockSpec((bm, k), lambda i, j: (i, 0)),
                  pl.BlockSpec((k, bn), lambda i, j: (0, j))],
        out_specs=pl.BlockSpec((bm, bn), lambda i, j: (i, j)),
        out_shape=jax.ShapeDtypeStruct((m, n), out_dtype),
        scratch_shapes=[pltpu.VMEM((bm, k), BF16)] if cast else [],
        compiler_params=_params(("parallel", "arbitrary"), vmem + (8 << 20)),
        name=name,
    )(x, w)


def _alibi_slope(h):
    return 2.0 ** (-8.0 * (h + 1) / N_HEADS_SWA)


_SWA_CHUNK = 32


def _swa_kernel(sink_ref, q_ref, kp_ref, kc_ref, vp_ref, vc_ref, g_ref, o_ref,
                qs_ref, k_s, vk_ref, s_ref, p_ref, rl_ref, pv_ref):
    kvp = pl.program_id(1)
    n = pl.program_id(2)
    ch = _SWA_CHUNK
    k_s[:BLOCK] = kp_ref[...]
    k_s[BLOCK:] = kc_ref[...]
    lane = lax.broadcasted_iota(jnp.int32, (BLOCK, LANES), 1)
    halves = [lane < HEAD_DIM, lane >= HEAD_DIM]

    def stack_heads(c):
        return [(2 * c, c), (2 * c + 1, c), (2 * c, 1 - c), (2 * c + 1, 1 - c)]

    for t in range(4):
        c = t // 2
        q_t = q_ref[:, t * LANES:(t + 1) * LANES]
        q_sw = pltpu.roll(q_t.astype(F32), HEAD_DIM, 1).astype(BF16)
        for blk, (tt, u) in enumerate(stack_heads(c)):
            if tt == t:
                src = q_t if u == c else q_sw
                qs_ref[c, blk * BLOCK:(blk + 1) * BLOCK, :] = (
                    jnp.where(halves[c], src, jnp.zeros_like(src)) * QK_SCALE)

    for c in range(2):
        s_ref[c] = _dot_nt(qs_ref[c], k_s[...])

    v = jnp.concatenate([vp_ref[...], vc_ref[...]], axis=0).astype(F32)
    v_sw = pltpu.roll(v, HEAD_DIM, 1)
    lane2 = lax.broadcasted_iota(jnp.int32, (2 * BLOCK, LANES), 1)
    for c in range(2):
        in_c = (lane2 >= HEAD_DIM) if c else (lane2 < HEAD_DIM)
        vk_ref[c, 0] = jnp.where(in_c, v, 0.0).astype(BF16)
        vk_ref[c, 1] = jnp.where(in_c, 0.0, v_sw).astype(BF16)

    qi = lax.broadcasted_iota(jnp.int32, (ch, 2 * BLOCK), 0)
    kj = lax.broadcasted_iota(jnp.int32, (ch, 2 * BLOCK), 1)
    pens = []
    for r0 in range(0, BLOCK, ch):
        dist = qi + (r0 + BLOCK) - kj
        valid = (dist >= 0) & (dist < WINDOW) & ((kj >= BLOCK) | (n > 0))
        pens.append(jnp.where(valid, dist.astype(F32), jnp.inf))

    for c in range(2):
        for blk, (t, u) in enumerate(stack_heads(c)):
            slope = jnp.where(kvp == 0, _alibi_slope(t * 2 + u), _alibi_slope(8 + t * 2 + u))
            sink = sink_ref[kvp * 8 + t * 2 + u]
            for i, pen in enumerate(pens):
                r0 = blk * BLOCK + i * ch
                rows = slice(r0, r0 + ch)
                s = s_ref[c, rows, :] - slope * pen
                m = jnp.maximum(jnp.max(s, axis=-1, keepdims=True), sink)
                p = jnp.exp(s - m)
                den = jnp.sum(p, axis=-1, keepdims=True) + jnp.exp(sink - m)
                p_ref[c, rows, :] = p.astype(BF16)
                rl_ref[c, rows, :] = jnp.broadcast_to(1.0 / den, (ch, LANES))
        half_rows = 2 * BLOCK
        pv_ref[c, :half_rows, :] = _dot(p_ref[c, :half_rows, :], vk_ref[c, 0])
        pv_ref[c, half_rows:, :] = _dot(p_ref[c, half_rows:, :], vk_ref[c, 1])

        for i in range(2):
            t = 2 * c + i
            y2 = None
            for blk in (i, 2 + i):
                rows = slice(blk * BLOCK, (blk + 1) * BLOCK)
                pv = pv_ref[c, rows, :]
                rl = rl_ref[c, rows, :]
                ss = jnp.sum(pv * pv, axis=-1, keepdims=True)
                y = pv * (rl * lax.rsqrt(rl * rl * ss / HEAD_DIM + RMS_EPS))
                y2 = y if y2 is None else y2 + y
            o_ref[:, t * LANES:(t + 1) * LANES] = (
                y2 * g_ref[:, t * LANES:(t + 1) * LANES]).astype(o_ref.dtype)


def _swa_attention(qkv, sinks, g_swa):
    b, s, _ = qkv.shape
    nb = s // BLOCK
    qw = 4 * LANES
    kv_spec = lambda blk, prev: pl.BlockSpec(
        (None, BLOCK, LANES),
        (lambda bi, kp, n: (bi, jnp.maximum(n - 1, 0), blk + kp)) if prev
        else (lambda bi, kp, n: (bi, n, blk + kp)))
    return pl.pallas_call(
        _swa_kernel,
        grid=(b, 2, nb),
        in_specs=[pl.BlockSpec(memory_space=pltpu.SMEM),
                  pl.BlockSpec((None, BLOCK, qw), lambda bi, kp, n: (bi, n, kp)),
                  kv_spec(_KA_BLK, True), kv_spec(_KA_BLK, False),
                  kv_spec(_VA_BLK, True), kv_spec(_VA_BLK, False),
                  pl.BlockSpec((1, qw), lambda bi, kp, n: (0, kp))],
        out_specs=pl.BlockSpec((None, BLOCK, qw), lambda bi, kp, n: (bi, n, kp)),
        out_shape=jax.ShapeDtypeStruct((b, s, D_SWA), BF16),
        scratch_shapes=[pltpu.VMEM((2, 4 * BLOCK, LANES), BF16),
                        pltpu.VMEM((2 * BLOCK, LANES), BF16),
                        pltpu.VMEM((2, 2, 2 * BLOCK, LANES), BF16),
                        pltpu.VMEM((2, 4 * BLOCK, 2 * BLOCK), F32),
                        pltpu.VMEM((2, 4 * BLOCK, 2 * BLOCK), BF16),
                        pltpu.VMEM((2, 4 * BLOCK, LANES), F32),
                        pltpu.VMEM((2, 4 * BLOCK, LANES), F32)],
        compiler_params=_params(("parallel", "parallel", "parallel"), 16 << 20),
        name="swa_attention",
    )(sinks, qkv, qkv, qkv, qkv, qkv, g_swa)


_SB_TK = 256
_SB_TQ = 512
_SB_CHUNK = 32
_SB_MASKED = -1e30
_LOG2E = 1.4426950408889634


def _sb_kernel(n_cast, q_ref, k_ref, v_ref, nu_ref, g_ref, *refs):
    w_refs, o_ref, wb_refs = refs[:n_cast], refs[n_cast], refs[n_cast + 1:2 * n_cast + 1]
    qm_ref, z_ref, pb_ref, bt_ref, a_ref, acc_ref, car_ref = refs[2 * n_cast + 1:]
    for w_ref, wb_ref in zip(w_refs, wb_refs):
        wb_ref[...] = w_ref[...].astype(BF16)

    tq, tk, ch = _SB_TQ, _SB_TK, _SB_CHUNK
    ntile = tq // tk
    lane = lax.broadcasted_iota(jnp.int32, (tq, LANES), 1)
    cmr = (lax.broadcasted_iota(jnp.int32, (ch, tk), 1)
           - lax.broadcasted_iota(jnp.int32, (ch, tk), 0))

    def key_tile(ref, j):
        return ref[pl.ds(pl.multiple_of(j * tk, tk), tk), :]

    def z_stage(slot, j, r_lo):
        k = key_tile(k_ref, j)
        for h in range(2):
            z_ref[slot, h, r_lo:, :] = _dot_nt(qm_ref[h, r_lo:, :], k)

    def p_stage(slot, h, off, r_lo):
        for r0 in range(r_lo, tq, ch):
            rows = slice(r0, r0 + ch)
            t = z_ref[slot, h, rows, :] * _LOG2E
            p = jnp.maximum(t, 0.0) + jnp.log2(1.0 + jnp.exp2(-jnp.abs(t)))
            car = car_ref[h, rows, :]
            tc = t + jnp.concatenate([car, car], axis=1)
            if off is not None:
                causal = cmr < off + r0
                p = jnp.where(causal, p, 0.0)
                tc = jnp.where(causal, tc, _SB_MASKED)
            pb_ref[h, rows, :] = p.astype(BF16)
            z_ref[slot, h, rows, :] = tc
            car_ref[h, rows, :] = car - jnp.sum(p, axis=-1, keepdims=True)

    def sum_stage(h, r_lo):
        bt_ref[h, r_lo:, :] = _dot(pb_ref[h, r_lo:, :], nu_ref[...])

    def w_stage(slot, h, r_lo):
        for r0 in range(r_lo, tq, ch):
            rows = slice(r0, r0 + ch)
            a_ref[h, rows, :] = jnp.exp2(z_ref[slot, h, rows, :] + bt_ref[h, rows, :]).astype(BF16)

    def pv_stage(j, r_lo):
        v = key_tile(v_ref, j)
        for h in range(2):
            acc_ref[h, r_lo:, :] += _dot(a_ref[h, r_lo:, :], v)

    def body(j, slot, off, r_lo=0, r_prev=0, r_next=0):
        if r_prev is not None:
            pv_stage(j + 1, r_prev)
        z_stage(1 - slot, jnp.maximum(j - 1, 0), r_next)
        p_stage(slot, 0, off, r_lo)
        sum_stage(0, r_lo)
        p_stage(slot, 1, off, r_lo)
        sum_stage(1, r_lo)
        w_stage(slot, 0, r_lo)
        w_stage(slot, 1, r_lo)

    assert ntile % 2 == 0
    first_row = [(ntile - 1 - i) * tk for i in range(ntile)] + [0]

    @pl.loop(0, q_ref.shape[0] // tq)
    def _(qb):
        q_rows = pl.ds(pl.multiple_of(qb * tq, tq), tq)
        last = qb * ntile + ntile - 1
        q = q_ref[q_rows, :] * QK_SCALE
        qm_ref[0] = jnp.where(lane < HEAD_DIM, q, jnp.zeros_like(q))
        qm_ref[1] = jnp.where(lane >= HEAD_DIM, q, jnp.zeros_like(q))
        acc_ref[...] = jnp.zeros_like(acc_ref)
        car_ref[...] = jnp.zeros_like(car_ref)

        z_stage(0, last, first_row[0])
        for i in range(ntile):
            body(last - i, i % 2, (i + 1 - ntile) * tk, first_row[i],
                 first_row[i - 1] if i else None, first_row[i + 1])

        @pl.loop(0, qb * (ntile // 2))
        def _(m):
            j = qb * ntile - 1 - 2 * m
            body(j, 0, None)
            body(j - 1, 1, None)

        pv_stage(0, 0)

        ys = []
        for h, half in enumerate([lane < HEAD_DIM, lane >= HEAD_DIM]):
            o = acc_ref[h]
            ms = jnp.sum(jnp.where(half, o * o, 0.0), axis=-1, keepdims=True) / HEAD_DIM
            ys.append(o * lax.rsqrt(ms + RMS_EPS))
        o_ref[q_rows, :] = (jnp.where(lane < HEAD_DIM, ys[0], ys[1]) * g_ref[...]).astype(o_ref.dtype)


_CAST_ROW_BLOCKS, _CAST_COL_BLOCKS = 8, 8


def _sb_attention(qkv, g_sb, weights):
    b, s, _ = qkv.shape
    npair = D_SB // LANES
    tq, tk = _SB_TQ, _SB_TK
    r = lax.broadcasted_iota(jnp.int32, (tk, tk), 0)
    c = lax.broadcasted_iota(jnp.int32, (tk, tk), 1)
    nu = -(r >= c).astype(BF16)

    assert b * npair == _CAST_ROW_BLOCKS * _CAST_COL_BLOCKS

    def cast_spec(w):
        rows, cols = w.shape
        assert rows % (16 * _CAST_ROW_BLOCKS) == 0 and cols % (LANES * _CAST_COL_BLOCKS) == 0

        def index(bi, p):
            step = bi * npair + p
            return (lax.shift_right_logical(step, _CAST_COL_BLOCKS.bit_length() - 1),
                    jnp.bitwise_and(step, _CAST_COL_BLOCKS - 1))
        return pl.BlockSpec((rows // _CAST_ROW_BLOCKS, cols // _CAST_COL_BLOCKS), index)

    cast_specs = [cast_spec(w) for w in weights]
    outs = pl.pallas_call(
        functools.partial(_sb_kernel, len(weights)),
        grid=(b, npair),
        in_specs=[pl.BlockSpec((None, s, LANES), lambda bi, p: (bi, 0, _QB_BLK + p)),
                  pl.BlockSpec((None, s, LANES), lambda bi, p: (bi, 0, _KB_BLK + p)),
                  pl.BlockSpec((None, s, LANES), lambda bi, p: (bi, 0, _VB_BLK + p)),
                  pl.BlockSpec((tk, tk), lambda bi, p: (0, 0)),
                  pl.BlockSpec((1, LANES), lambda bi, p: (0, p))] + cast_specs,
        out_specs=[pl.BlockSpec((None, s, LANES), lambda bi, p: (bi, 0, p))] + cast_specs,
        out_shape=[jax.ShapeDtypeStruct((b, s, D_SB), BF16)]
        + [jax.ShapeDtypeStruct(w.shape, BF16) for w in weights],
        scratch_shapes=[pltpu.VMEM((2, tq, LANES), BF16),
                        pltpu.VMEM((2, 2, tq, tk), F32),
                        pltpu.VMEM((2, tq, tk), BF16),
                        pltpu.VMEM((2, tq, tk), F32),
                        pltpu.VMEM((2, tq, tk), BF16),
                        pltpu.VMEM((2, tq, LANES), F32),
                        pltpu.VMEM((2, tq, LANES), F32)],
        compiler_params=_params(("parallel", "parallel"), 32 << 20),
        name="sb_attention",
    )(qkv, qkv, qkv, nu, g_sb, *weights)
    return outs[0], outs[1:]


_LN_GROUPS = 4


def _proj_ln_q_kernel(oa_ref, ob_ref, wo_ref, x_ref, g_ref, b_ref, wq_ref, h_ref, q_ref,
                      mix_ref, hb_ref):
    bm = x_ref.shape[0]
    groups = [slice(r, r + bm // _LN_GROUPS) for r in range(0, bm, bm // _LN_GROUPS)]
    for rows in groups:
        mix_ref[rows, :] = (_dot(oa_ref[rows, :], wo_ref[:D_SWA, :])
                            + _dot(ob_ref[rows, :], wo_ref[D_SWA:, :]))
    for rows in groups:
        h = _layer_norm(ALPHA * x_ref[rows, :] + mix_ref[rows, :], g_ref[...], b_ref[...])
        h_ref[rows, :] = h
        hb_ref[rows, :] = h.astype(BF16)
    for rows in groups:
        q_ref[rows, :] = _dot(hb_ref[rows, :], wq_ref[...]).astype(BF16)


def _proj_ln_q(o_a, o_b, w_o, x, g, b, w_q, bm):
    m, d = x.shape
    row = lambda w: pl.BlockSpec((bm, w), lambda i: (i, 0))
    full = lambda r, w: pl.BlockSpec((r, w), lambda i: (0, 0), pipeline_mode=pl.Buffered(1))
    vmem = (2 * (2 * bm * D_SWA * 2 + bm * d * (4 + 4 + 2)) + 2 * d * d * 2
            + bm * d * (4 + 2) + 2 * bm * d * 4)
    return pl.pallas_call(
        _proj_ln_q_kernel,
        grid=(m // bm,),
        in_specs=[row(D_SWA), row(D_SB), full(d, d), row(d), full(1, d), full(1, d), full(d, d)],
        out_specs=[row(d), row(d)],
        out_shape=[jax.ShapeDtypeStruct((m, d), F32), jax.ShapeDtypeStruct((m, d), BF16)],
        scratch_shapes=[pltpu.VMEM((bm, d), F32), pltpu.VMEM((bm, d), BF16)],
        compiler_params=_params(("parallel",), vmem),
        name="mix_proj_ln_q",
    )(o_a, o_b, w_o, x, g, b, w_q)


_XATTN_CHUNK = 32


def _xattn_kernel(q_ref, kv_ref, w_ref, h_ref, g_ref, b_ref, o_ref, ob_ref,
                  s_ref, p_ref, rl_ref, c_ref, y_ref):
    bm = q_ref.shape[0]
    ch = _XATTN_CHUNK
    scale = 1.0 / math.sqrt(HEAD_DIM_MEM)
    heads = [slice(hd * HEAD_DIM_MEM, (hd + 1) * HEAD_DIM_MEM) for hd in range(N_HEADS_MEM)]
    for hd, cols in enumerate(heads):
        s_ref[hd] = _dot_nt(q_ref[:, cols], kv_ref[:, cols])
    for hd, cols in enumerate(heads):
        for r0 in range(0, bm, ch):
            rows = slice(r0, r0 + ch)
            s = s_ref[hd, rows, :] * scale
            p = jnp.exp(s - jnp.max(s, axis=-1, keepdims=True))
            l = jnp.sum(p, axis=-1, keepdims=True)
            p_ref[hd, rows, :] = p.astype(BF16)
            rl_ref[hd, rows, :] = jnp.broadcast_to(1.0 / l, (ch, LANES))
        o = _dot(p_ref[hd], kv_ref[:, D_MODEL + cols.start:D_MODEL + cols.stop])
        rl = rl_ref[hd]
        c_ref[:, cols] = (o * jnp.concatenate([rl] * (HEAD_DIM_MEM // LANES), axis=1)).astype(BF16)
    groups = [slice(r, r + bm // _LN_GROUPS) for r in range(0, bm, bm // _LN_GROUPS)]
    for rows in groups:
        y_ref[rows, :] = _dot(c_ref[rows, :], w_ref[...])
    for rows in groups:
        h = _layer_norm(ALPHA * h_ref[rows, :] + y_ref[rows, :], g_ref[...], b_ref[...])
        o_ref[rows, :] = h
        ob_ref[rows, :] = h.astype(BF16)


def _xattn_ln(q, kv, w_o, h, g, b, bm):
    bsz, s, d = q.shape
    mlen = kv.shape[1]
    row = lambda: pl.BlockSpec((None, bm, d), lambda bi, i: (bi, i, 0))
    full = lambda r, w: pl.BlockSpec((r, w), lambda bi, i: (0, 0), pipeline_mode=pl.Buffered(1))
    vmem = (2 * (bm * d * 2 + mlen * 2 * d * 2 + bm * d * (4 + 4 + 2)) + d * d * 2
            + N_HEADS_MEM * bm * (mlen * 6 + LANES * 4) + bm * d * (2 + 4) + 2 * bm * d * 4)
    return pl.pallas_call(
        _xattn_kernel,
        grid=(bsz, s // bm),
        in_specs=[row(), pl.BlockSpec((None, mlen, 2 * d), lambda bi, i: (bi, 0, 0)),
                  full(d, d), row(), full(1, d), full(1, d)],
        out_specs=[row(), row()],
        out_shape=[jax.ShapeDtypeStruct((bsz, s, d), F32),
                   jax.ShapeDtypeStruct((bsz, s, d), BF16)],
        scratch_shapes=[pltpu.VMEM((N_HEADS_MEM, bm, mlen), F32),
                        pltpu.VMEM((N_HEADS_MEM, bm, mlen), BF16),
                        pltpu.VMEM((N_HEADS_MEM, bm, LANES), F32),
                        pltpu.VMEM((bm, d), BF16),
                        pltpu.VMEM((bm, d), F32)],
        compiler_params=_params(("parallel", "parallel"), vmem),
        name="mem_xattn_ln",
    )(q, kv, w_o, h, g, b)


def _ffn_kernel(xb_ref, wg_ref, wu_ref, wd_ref, h_ref, g_ref, b_ref, o_ref, acc_ref, act_ref):
    j = pl.program_id(1)
    last = pl.num_programs(1) - 1

    def activations():
        x = xb_ref[...]
        gate = _dot(x, wg_ref[...])
        up = _dot(x, wu_ref[...])
        return (gate / (1.0 + jnp.exp(-gate)) * up).astype(BF16)

    @pl.when(j == 0)
    def _():
        acc_ref[...] = _dot(activations(), wd_ref[...])

    @pl.when((j > 0) & (j < last))
    def _():
        acc_ref[...] += _dot(activations(), wd_ref[...])

    @pl.when(j == last)
    def _():
        act_ref[...] = activations()
        bm = act_ref.shape[0]
        groups = [slice(r, r + bm // _LN_GROUPS) for r in range(0, bm, bm // _LN_GROUPS)]
        for rows in groups:
            acc_ref[rows, :] += _dot(act_ref[rows, :], wd_ref[...])
        for rows in groups:
            o_ref[rows, :] = _layer_norm(ALPHA * h_ref[rows, :] + acc_ref[rows, :],
                                         g_ref[...], b_ref[...])


def _ffn_ln(hb, w_gu, w_d, h, g, b, bm, bf):
    m, d = h.shape
    dff = w_d.shape[0]
    nf = dff // bf
    assert dff % bf == 0 and m % bm == 0 and nf >= 2
    row = lambda: pl.BlockSpec((bm, d), lambda i, j: (i, 0))
    full = lambda: pl.BlockSpec((1, d), lambda i, j: (0, 0))
    vmem = 2 * (bm * d * 2 + 3 * d * bf * 2 + 2 * bm * d * 4) + bm * d * 4 + 4 * bm * bf * 4
    return pl.pallas_call(
        _ffn_kernel,
        grid=(m // bm, nf),
        in_specs=[row(),
                  pl.BlockSpec((d, bf), lambda i, j: (0, j)),
                  pl.BlockSpec((d, bf), lambda i, j: (0, j + nf)),
                  pl.BlockSpec((bf, d), lambda i, j: (j, 0)),
                  row(), full(), full()],
        out_specs=row(),
        out_shape=jax.ShapeDtypeStruct((m, d), F32),
        scratch_shapes=[pltpu.VMEM((bm, d), F32), pltpu.VMEM((bm, bf), BF16)],
        compiler_params=_params(("parallel", "arbitrary"), vmem + (4 << 20)),
        name="swiglu_ffn_ln",
    )(hb, w_gu, w_gu, w_d, h, g, b)


_QKV_BLOCK = (1024, 1536)
_KV_BLOCK = (1024, 1024)
_LN_BLOCK_ROWS = 512
_FFN_BLOCK_COLS = 512


def kernel(x, mem, w_in, sinks, g_swa, g_sb, w_o, ln1_g, ln1_b, w_q_mem, w_kv_mem,
           w_o_mem, ln2_g, ln2_b, w_gate_up, w_down, ln3_g, ln3_b):
    bsz, s, d = x.shape
    t = bsz * s
    mlen = mem.shape[1]
    h = x.reshape(t, d)
    for l in range(DEPTH):
        row = lambda a: a[l].reshape(1, -1)
        qkv = _matmul(h, w_in[l].astype(BF16), *_QKV_BLOCK, BF16, "qkv_proj")
        qkv = qkv.reshape(bsz, s, D_IN)
        o_a = _swa_attention(qkv, sinks[l], row(g_swa))
        o_b, (wo_b, wq_b, wkv_b, wom_b, wgu_b, wd_b) = _sb_attention(
            qkv, row(g_sb),
            [w_o[l], w_q_mem[l], w_kv_mem[l], w_o_mem[l], w_gate_up[l], w_down[l]])
        h, q = _proj_ln_q(o_a.reshape(t, D_SWA), o_b.reshape(t, D_SB), wo_b,
                          h, row(ln1_g), row(ln1_b), wq_b, _LN_BLOCK_ROWS)
        kv = _matmul(mem.reshape(bsz * mlen, d), wkv_b, *_KV_BLOCK, BF16, "mem_kv_proj")
        h3, hb3 = _xattn_ln(q.reshape(bsz, s, d), kv.reshape(bsz, mlen, 2 * d),
                            wom_b, h.reshape(bsz, s, d), row(ln2_g), row(ln2_b), _LN_BLOCK_ROWS)
        h, hb = h3.reshape(t, d), hb3.reshape(t, d)
        h = _ffn_ln(hb, wgu_b, wd_b, h, row(ln3_g), row(ln3_b), _LN_BLOCK_ROWS, _FFN_BLOCK_COLS)
    return h.reshape(bsz, s, d)
```

```python
import functools
import math

import jax
import jax.numpy as jnp
from jax import lax
from jax.experimental import pallas as pl
from jax.experimental.pallas import tpu as pltpu

F32 = jnp.float32
BF16 = jnp.bfloat16

D_MODEL = 2048
HEAD_DIM = 64
N_HEADS_SWA = 16
N_KV_SWA = 4
N_HEADS_SB = 16
WINDOW = 128
BLOCK = 128
N_HEADS_MEM = 4
HEAD_DIM_MEM = D_MODEL // N_HEADS_MEM
D_SWA = N_HEADS_SWA * HEAD_DIM
D_KV_SWA = N_KV_SWA * HEAD_DIM
D_SB = N_HEADS_SB * HEAD_DIM
D_IN = D_SWA + 2 * D_KV_SWA + 3 * D_SB
DEPTH = 1
ALPHA = (2.0 * DEPTH) ** 0.25
LN_EPS = 1e-5
RMS_EPS = 1e-6
QK_SCALE = 1.0 / math.sqrt(HEAD_DIM)

LANES = 128
V7X_VMEM_LIMIT = 56 * 1024 * 1024

_KA_BLK = D_SWA // LANES
_VA_BLK = (D_SWA + D_KV_SWA) // LANES
_QB_BLK = (D_SWA + 2 * D_KV_SWA) // LANES
_KB_BLK = _QB_BLK + D_SB // LANES
_VB_BLK = _KB_BLK + D_SB // LANES


def _params(sem, vmem_bytes, flags=None):
    return pltpu.CompilerParams(
        dimension_semantics=sem, vmem_limit_bytes=min(int(vmem_bytes), V7X_VMEM_LIMIT),
        flags=flags)


def _dot(a, b):
    return jnp.dot(a, b, preferred_element_type=F32)


def _dot_nt(a, b):
    return lax.dot_general(a, b, (((1,), (1,)), ((), ())), preferred_element_type=F32)


def _layer_norm(y, g, b):
    mu = jnp.mean(y, axis=-1, keepdims=True)
    d = y - mu
    var = jnp.mean(d * d, axis=-1, keepdims=True)
    return d * lax.rsqrt(var + LN_EPS) * g + b


def _mm_kernel(x_ref, w_ref, o_ref, *xb_ref):
    if xb_ref:
        @pl.when(pl.program_id(1) == 0)
        def _():
            xb_ref[0][...] = x_ref[...].astype(BF16)
        x = xb_ref[0][...]
    else:
        x = x_ref[...]
    o_ref[...] = _dot(x, w_ref[...]).astype(o_ref.dtype)


def _matmul(x, w, bm, bn, out_dtype, name):
    m, k = x.shape
    n = w.shape[1]
    assert m % bm == 0 and n % bn == 0
    osz = jnp.dtype(out_dtype).itemsize
    xsz = jnp.dtype(x.dtype).itemsize
    cast = x.dtype != BF16
    vmem = 2 * (bm * k * xsz + k * bn * 2 + bm * bn * osz) + bm * bn * 4 + cast * bm * k * 2
    return pl.pallas_call(
        _mm_kernel,
        grid=(m // bm, n // bn),
        in_specs=[pl.BlockSpec((bm, k), lambda i, j: (i, 0)),
                  pl.BlockSpec((k, bn), lambda i, j: (0, j))],
        out_specs=pl.BlockSpec((bm, bn), lambda i, j: (i, j)),
        out_shape=jax.ShapeDtypeStruct((m, n), out_dtype),
        scratch_shapes=[pltpu.VMEM((bm, k), BF16)] if cast else [],
        compiler_params=_params(("parallel", "arbitrary"), vmem + (8 << 20)),
        name=name,
    )(x, w)


def _alibi_slope(h):
    return 2.0 ** (-8.0 * (h + 1) / N_HEADS_SWA)


_SWA_CHUNK = 32


def _swa_kernel(sink_ref, q_ref, kp_ref, kc_ref, vp_ref, vc_ref, g_ref, o_ref,
                qs_ref, k_s, vk_ref, s_ref, p_ref, rl_ref, pv_ref):
    kvp = pl.program_id(1)
    n = pl.program_id(2)
    ch = _SWA_CHUNK
    k_s[:BLOCK] = kp_ref[...]
    k_s[BLOCK:] = kc_ref[...]
    lane = lax.broadcasted_iota(jnp.int32, (BLOCK, LANES), 1)
    halves = [lane < HEAD_DIM, lane >= HEAD_DIM]

    def stack_heads(c):
        return [(2 * c, c), (2 * c + 1, c), (2 * c, 1 - c), (2 * c + 1, 1 - c)]

    for t in range(4):
        c = t // 2
        q_t = q_ref[:, t * LANES:(t + 1) * LANES]
        q_sw = pltpu.roll(q_t.astype(F32), HEAD_DIM, 1).astype(BF16)
        for blk, (tt, u) in enumerate(stack_heads(c)):
            if tt == t:
                src = q_t if u == c else q_sw
                qs_ref[c, blk * BLOCK:(blk + 1) * BLOCK, :] = (
                    jnp.where(halves[c], src, jnp.zeros_like(src)) * QK_SCALE)

    for c in range(2):
        s_ref[c] = _dot_nt(qs_ref[c], k_s[...])

    v = jnp.concatenate([vp_ref[...], vc_ref[...]], axis=0).astype(F32)
    v_sw = pltpu.roll(v, HEAD_DIM, 1)
    lane2 = lax.broadcasted_iota(jnp.int32, (2 * BLOCK, LANES), 1)
    for c in range(2):
        in_c = (lane2 >= HEAD_DIM) if c else (lane2 < HEAD_DIM)
        vk_ref[c, 0] = jnp.where(in_c, v, 0.0).astype(BF16)
        vk_ref[c, 1] = jnp.where(in_c, 0.0, v_sw).astype(BF16)

    qi = lax.broadcasted_iota(jnp.int32, (ch, 2 * BLOCK), 0)
    kj = lax.broadcasted_iota(jnp.int32, (ch, 2 * BLOCK), 1)
    pens = []
    for r0 in range(0, BLOCK, ch):
        dist = qi + (r0 + BLOCK) - kj
        valid = (dist >= 0) & (dist < WINDOW) & ((kj >= BLOCK) | (n > 0))
        pens.append(jnp.where(valid, dist.astype(F32), jnp.inf))

    for c in range(2):
        for blk, (t, u) in enumerate(stack_heads(c)):
            slope = jnp.where(kvp == 0, _alibi_slope(t * 2 + u), _alibi_slope(8 + t * 2 + u))
            sink = sink_ref[kvp * 8 + t * 2 + u]
            for i, pen in enumerate(pens):
                r0 = blk * BLOCK + i * ch
                rows = slice(r0, r0 + ch)
                s = s_ref[c, rows, :] - slope * pen
                m = jnp.maximum(jnp.max(s, axis=-1, keepdims=True), sink)
                p = jnp.exp(s - m)
                den = jnp.sum(p, axis=-1, keepdims=True) + jnp.exp(sink - m)
                p_ref[c, rows, :] = p.astype(BF16)
                rl_ref[c, rows, :] = jnp.broadcast_to(1.0 / den, (ch, LANES))
        half_rows = 2 * BLOCK
        pv_ref[c, :half_rows, :] = _dot(p_ref[c, :half_rows, :], vk_ref[c, 0])
        pv_ref[c, half_rows:, :] = _dot(p_ref[c, half_rows:, :], vk_ref[c, 1])

        for i in range(2):
            t = 2 * c + i
            y2 = None
            for blk in (i, 2 + i):
                rows = slice(blk * BLOCK, (blk + 1) * BLOCK)
                pv = pv_ref[c, rows, :]
                rl = rl_ref[c, rows, :]
                ss = jnp.sum(pv * pv, axis=-1, keepdims=True)
                y = pv * (rl * lax.rsqrt(rl * rl * ss / HEAD_DIM + RMS_EPS))
                y2 = y if y2 is None else y2 + y
            o_ref[:, t * LANES:(t + 1) * LANES] = (
                y2 * g_ref[:, t * LANES:(t + 1) * LANES]).astype(o_ref.dtype)


def _swa_attention(qkv, sinks, g_swa):
    b, s, _ = qkv.shape
    nb = s // BLOCK
    qw = 4 * LANES
    kv_spec = lambda blk, prev: pl.BlockSpec(
        (None, BLOCK, LANES),
        (lambda bi, kp, n: (bi, jnp.maximum(n - 1, 0), blk + kp)) if prev
        else (lambda bi, kp, n: (bi, n, blk + kp)))
    return pl.pallas_call(
        _swa_kernel,
        grid=(b, 2, nb),
        in_specs=[pl.BlockSpec(memory_space=pltpu.SMEM),
                  pl.BlockSpec((None, BLOCK, qw), lambda bi, kp, n: (bi, n, kp)),
                  kv_spec(_KA_BLK, True), kv_spec(_KA_BLK, False),
                  kv_spec(_VA_BLK, True), kv_spec(_VA_BLK, False),
                  pl.BlockSpec((1, qw), lambda bi, kp, n: (0, kp))],
        out_specs=pl.BlockSpec((None, BLOCK, qw), lambda bi, kp, n: (bi, n, kp)),
        out_shape=jax.ShapeDtypeStruct((b, s, D_SWA), BF16),
        scratch_shapes=[pltpu.VMEM((2, 4 * BLOCK, LANES), BF16),
                        pltpu.VMEM((2 * BLOCK, LANES), BF16),
                        pltpu.VMEM((2, 2, 2 * BLOCK, LANES), BF16),
                        pltpu.VMEM((2, 4 * BLOCK, 2 * BLOCK), F32),
                        pltpu.VMEM((2, 4 * BLOCK, 2 * BLOCK), BF16),
                        pltpu.VMEM((2, 4 * BLOCK, LANES), F32),
                        pltpu.VMEM((2, 4 * BLOCK, LANES), F32)],
        compiler_params=_params(("parallel", "parallel", "parallel"), 16 << 20),
        name="swa_attention",
    )(sinks, qkv, qkv, qkv, qkv, qkv, g_swa)


_SB_TK = 256
_SB_TQ = 512
_SB_CHUNK = 32
_SB_MASKED = -1e30
_LOG2E = 1.4426950408889634


def _sb_kernel(n_cast, q_ref, k_ref, v_ref, nu_ref, g_ref, *refs):
    w_refs, o_ref, wb_refs = refs[:n_cast], refs[n_cast], refs[n_cast + 1:2 * n_cast + 1]
    qm_ref, z_ref, pb_ref, bt_ref, a_ref, acc_ref, car_ref = refs[2 * n_cast + 1:]
    for w_ref, wb_ref in zip(w_refs, wb_refs):
        wb_ref[...] = w_ref[...].astype(BF16)

    tq, tk, ch = _SB_TQ, _SB_TK, _SB_CHUNK
    ntile = tq // tk
    lane = lax.broadcasted_iota(jnp.int32, (tq, LANES), 1)
    cmr = (lax.broadcasted_iota(jnp.int32, (ch, tk), 1)
           - lax.broadcasted_iota(jnp.int32, (ch, tk), 0))

    def key_tile(ref, j):
        return ref[pl.ds(pl.multiple_of(j * tk, tk), tk), :]

    def z_stage(slot, j, r_lo):
        k = key_tile(k_ref, j)
        for h in range(2):
            z_ref[slot, h, r_lo:, :] = _dot_nt(qm_ref[h, r_lo:, :], k)

    def p_stage(slot, h, off, r_lo):
        for r0 in range(r_lo, tq, ch):
            rows = slice(r0, r0 + ch)
            t = z_ref[slot, h, rows, :] * _LOG2E
            p = jnp.maximum(t, 0.0) + jnp.log2(1.0 + jnp.exp2(-jnp.abs(t)))
            car = car_ref[h, rows, :]
            tc = t + jnp.concatenate([car, car], axis=1)
            if off is not None:
                causal = cmr < off + r0
                p = jnp.where(causal, p, 0.0)
                tc = jnp.where(causal, tc, _SB_MASKED)
            pb_ref[h, rows, :] = p.astype(BF16)
            z_ref[slot, h, rows, :] = tc
            car_ref[h, rows, :] = car - jnp.sum(p, axis=-1, keepdims=True)

    def sum_stage(h, r_lo):
        bt_ref[h, r_lo:, :] = _dot(pb_ref[h, r_lo:, :], nu_ref[...])

    def w_stage(slot, h, r_lo):
        for r0 in range(r_lo, tq, ch):
            rows = slice(r0, r0 + ch)
            a_ref[h, rows, :] = jnp.exp2(z_ref[slot, h, rows, :] + bt_ref[h, rows, :]).astype(BF16)

    def pv_stage(j, r_lo):
        v = key_tile(v_ref, j)
        for h in range(2):
            acc_ref[h, r_lo:, :] += _dot(a_ref[h, r_lo:, :], v)

    def body(j, slot, off, r_lo=0, r_prev=0, r_next=0):
        if r_prev is not None:
            pv_stage(j + 1, r_prev)
        z_stage(1 - slot, jnp.maximum(j - 1, 0), r_next)
        p_stage(slot, 0, off, r_lo)
        sum_stage(0, r_lo)
        p_stage(slot, 1, off, r_lo)
        sum_stage(1, r_lo)
        w_stage(slot, 0, r_lo)
        w_stage(slot, 1, r_lo)

    assert ntile % 2 == 0
    first_row = [(ntile - 1 - i) * tk for i in range(ntile)] + [0]

    @pl.loop(0, q_ref.shape[0] // tq)
    def _(qb):
        q_rows = pl.ds(pl.multiple_of(qb * tq, tq), tq)
        last = qb * ntile + ntile - 1
        q = q_ref[q_rows, :] * QK_SCALE
        qm_ref[0] = jnp.where(lane < HEAD_DIM, q, jnp.zeros_like(q))
        qm_ref[1] = jnp.where(lane >= HEAD_DIM, q, jnp.zeros_like(q))
        acc_ref[...] = jnp.zeros_like(acc_ref)
        car_ref[...] = jnp.zeros_like(car_ref)

        z_stage(0, last, first_row[0])
        for i in range(ntile):
            body(last - i, i % 2, (i + 1 - ntile) * tk, first_row[i],
                 first_row[i - 1] if i else None, first_row[i + 1])

        @pl.loop(0, qb * (ntile // 2))
        def _(m):
            j = qb * ntile - 1 - 2 * m
            body(j, 0, None)
            body(j - 1, 1, None)

        pv_stage(0, 0)

        ys = []
        for h, half in enumerate([lane < HEAD_DIM, lane >= HEAD_DIM]):
            o = acc_ref[h]
            ms = jnp.sum(jnp.where(half, o * o, 0.0), axis=-1, keepdims=True) / HEAD_DIM
            ys.append(o * lax.rsqrt(ms + RMS_EPS))
        o_ref[q_rows, :] = (jnp.where(lane < HEAD_DIM, ys[0], ys[1]) * g_ref[...]).astype(o_ref.dtype)


_CAST_ROW_BLOCKS, _CAST_COL_BLOCKS = 8, 8


def _sb_attention(qkv, g_sb, weights):
    b, s, _ = qkv.shape
    npair = D_SB // LANES
    tq, tk = _SB_TQ, _SB_TK
    r = lax.broadcasted_iota(jnp.int32, (tk, tk), 0)
    c = lax.broadcasted_iota(jnp.int32, (tk, tk), 1)
    nu = -(r >= c).astype(BF16)

    assert b * npair == _CAST_ROW_BLOCKS * _CAST_COL_BLOCKS

    def cast_spec(w):
        rows, cols = w.shape
        assert rows % (16 * _CAST_ROW_BLOCKS) == 0 and cols % (LANES * _CAST_COL_BLOCKS) == 0

        def index(bi, p):
            step = bi * npair + p
            return (lax.shift_right_logical(step, _CAST_COL_BLOCKS.bit_length() - 1),
                    jnp.bitwise_and(step, _CAST_COL_BLOCKS - 1))
        return pl.BlockSpec((rows // _CAST_ROW_BLOCKS, cols // _CAST_COL_BLOCKS), index)

    cast_specs = [cast_spec(w) for w in weights]
    outs = pl.pallas_call(
        functools.partial(_sb_kernel, len(weights)),
        grid=(b, npair),
        in_specs=[pl.BlockSpec((None, s, LANES), lambda bi, p: (bi, 0, _QB_BLK + p)),
                  pl.BlockSpec((None, s, LANES), lambda bi, p: (bi, 0, _KB_BLK + p)),
                  pl.BlockSpec((None, s, LANES), lambda bi, p: (bi, 0, _VB_BLK + p)),
                  pl.BlockSpec((tk, tk), lambda bi, p: (0, 0)),
                  pl.BlockSpec((1, LANES), lambda bi, p: (0, p))] + cast_specs,
        out_specs=[pl.BlockSpec((None, s, LANES), lambda bi, p: (bi, 0, p))] + cast_specs,
        out_shape=[jax.ShapeDtypeStruct((b, s, D_SB), BF16)]
        + [jax.ShapeDtypeStruct(w.shape, BF16) for w in weights],
        scratch_shapes=[pltpu.VMEM((2, tq, LANES), BF16),
                        pltpu.VMEM((2, 2, tq, tk), F32),
                        pltpu.VMEM((2, tq, tk), BF16),
                        pltpu.VMEM((2, tq, tk), F32),
                        pltpu.VMEM((2, tq, tk), BF16),
                        pltpu.VMEM((2, tq, LANES), F32),
                        pltpu.VMEM((2, tq, LANES), F32)],
        compiler_params=_params(("parallel", "parallel"), 32 << 20),
        name="sb_attention",
    )(qkv, qkv, qkv, nu, g_sb, *weights)
    return outs[0], outs[1:]


_LN_GROUPS = 4


def _proj_ln_q_kernel(oa_ref, ob_ref, wo_ref, x_ref, g_ref, b_ref, wq_ref, h_ref, q_ref,
                      mix_ref, hb_ref):
    bm = x_ref.shape[0]
    groups = [slice(r, r + bm // _LN_GROUPS) for r in range(0, bm, bm // _LN_GROUPS)]
    for rows in groups:
        mix_ref[rows, :] = (_dot(oa_ref[rows, :], wo_ref[:D_SWA, :])
                            + _dot(ob_ref[rows, :], wo_ref[D_SWA:, :]))
    for rows in groups:
        h = _layer_norm(ALPHA * x_ref[rows, :] + mix_ref[rows, :], g_ref[...], b_ref[...])
        h_ref[rows, :] = h
        hb_ref[rows, :] = h.astype(BF16)
    for rows in groups:
        q_ref[rows, :] = _dot(hb_ref[rows, :], wq_ref[...]).astype(BF16)


def _proj_ln_q(o_a, o_b, w_o, x, g, b, w_q, bm):
    m, d = x.shape
    row = lambda w: pl.BlockSpec((bm, w), lambda i: (i, 0))
    full = lambda r, w: pl.BlockSpec((r, w), lambda i: (0, 0), pipeline_mode=pl.Buffered(1))
    vmem = (2 * (2 * bm * D_SWA * 2 + bm * d * (4 + 4 + 2)) + 2 * d * d * 2
            + bm * d * (4 + 2) + 2 * bm * d * 4)
    return pl.pallas_call(
        _proj_ln_q_kernel,
        grid=(m // bm,),
        in_specs=[row(D_SWA), row(D_SB), full(d, d), row(d), full(1, d), full(1, d), full(d, d)],
        out_specs=[row(d), row(d)],
        out_shape=[jax.ShapeDtypeStruct((m, d), F32), jax.ShapeDtypeStruct((m, d), BF16)],
        scratch_shapes=[pltpu.VMEM((bm, d), F32), pltpu.VMEM((bm, d), BF16)],
        compiler_params=_params(("parallel",), vmem),
        name="mix_proj_ln_q",
    )(o_a, o_b, w_o, x, g, b, w_q)


_XATTN_CHUNK = 32


def _xattn_kernel(q_ref, kv_ref, w_ref, h_ref, g_ref, b_ref, o_ref,
                  s_ref, p_ref, rl_ref, c_ref, y_ref):
    bm = q_ref.shape[0]
    ch = _XATTN_CHUNK
    scale = 1.0 / math.sqrt(HEAD_DIM_MEM)
    heads = [slice(hd * HEAD_DIM_MEM, (hd + 1) * HEAD_DIM_MEM) for hd in range(N_HEADS_MEM)]
    for hd, cols in enumerate(heads):
        s_ref[hd] = _dot_nt(q_ref[:, cols], kv_ref[:, cols])
    for hd, cols in enumerate(heads):
        for r0 in range(0, bm, ch):
            rows = slice(r0, r0 + ch)
            s = s_ref[hd, rows, :] * scale
            p = jnp.exp(s - jnp.max(s, axis=-1, keepdims=True))
            l = jnp.sum(p, axis=-1, keepdims=True)
            p_ref[hd, rows, :] = p.astype(BF16)
            rl_ref[hd, rows, :] = jnp.broadcast_to(1.0 / l, (ch, LANES))
        o = _dot(p_ref[hd], kv_ref[:, D_MODEL + cols.start:D_MODEL + cols.stop])
        rl = rl_ref[hd]
        c_ref[:, cols] = (o * jnp.concatenate([rl] * (HEAD_DIM_MEM // LANES), axis=1)).astype(BF16)
    groups = [slice(r, r + bm // _LN_GROUPS) for r in range(0, bm, bm // _LN_GROUPS)]
    for rows in groups:
        y_ref[rows, :] = _dot(c_ref[rows, :], w_ref[...])
    for rows in groups:
        h = _layer_norm(ALPHA * h_ref[rows, :] + y_ref[rows, :], g_ref[...], b_ref[...])
        o_ref[rows, :] = h


def _xattn_ln(q, kv, w_o, h, g, b, bm):
    bsz, s, d = q.shape
    mlen = kv.shape[1]
    row = lambda: pl.BlockSpec((None, bm, d), lambda bi, i: (bi, i, 0))
    full = lambda r, w: pl.BlockSpec((r, w), lambda bi, i: (0, 0), pipeline_mode=pl.Buffered(1))
    vmem = (2 * (bm * d * 2 + mlen * 2 * d * 2 + bm * d * (4 + 4 + 2)) + d * d * 2
            + N_HEADS_MEM * bm * (mlen * 6 + LANES * 4) + bm * d * (2 + 4) + 2 * bm * d * 4)
    return pl.pallas_call(
        _xattn_kernel,
        grid=(bsz, s // bm),
        in_specs=[row(), pl.BlockSpec((None, mlen, 2 * d), lambda bi, i: (bi, 0, 0)),
                  full(d, d), row(), full(1, d), full(1, d)],
        out_specs=row(),
        out_shape=jax.ShapeDtypeStruct((bsz, s, d), F32),
        scratch_shapes=[pltpu.VMEM((N_HEADS_MEM, bm, mlen), F32),
                        pltpu.VMEM((N_HEADS_MEM, bm, mlen), BF16),
                        pltpu.VMEM((N_HEADS_MEM, bm, LANES), F32),
                        pltpu.VMEM((bm, d), BF16),
                        pltpu.VMEM((bm, d), F32)],
        compiler_params=_params(("parallel", "parallel"), vmem),
        name="mem_xattn_ln",
    )(q, kv, w_o, h, g, b)


def _ffn_kernel(wg_ref, wu_ref, wd_ref, h_ref, g_ref, b_ref, o_ref, acc_ref, act_ref, xb_ref):
    j = pl.program_id(1)
    last = pl.num_programs(1) - 1

    def activations():
        x = xb_ref[...]
        gate = _dot(x, wg_ref[...])
        up = _dot(x, wu_ref[...])
        return (gate / (1.0 + jnp.exp(-gate)) * up).astype(BF16)

    @pl.when(j == 0)
    def _():
        xb_ref[...] = h_ref[...].astype(BF16)
        acc_ref[...] = _dot(activations(), wd_ref[...])

    @pl.when((j > 0) & (j < last))
    def _():
        acc_ref[...] += _dot(activations(), wd_ref[...])

    @pl.when(j == last)
    def _():
        act_ref[...] = activations()
        bm = act_ref.shape[0]
        groups = [slice(r, r + bm // _LN_GROUPS) for r in range(0, bm, bm // _LN_GROUPS)]
        for rows in groups:
            acc_ref[rows, :] += _dot(act_ref[rows, :], wd_ref[...])
        for rows in groups:
            o_ref[rows, :] = _layer_norm(ALPHA * h_ref[rows, :] + acc_ref[rows, :],
                                         g_ref[...], b_ref[...])


def _ffn_ln(w_gu, w_d, h, g, b, bm, bf):
    m, d = h.shape
    dff = w_d.shape[0]
    nf = dff // bf
    assert dff % bf == 0 and m % bm == 0 and nf >= 2
    row = lambda: pl.BlockSpec((bm, d), lambda i, j: (i, 0))
    full = lambda: pl.BlockSpec((1, d), lambda i, j: (0, 0))
    vmem = 2 * (bm * d * 2 + 3 * d * bf * 2 + 2 * bm * d * 4) + bm * d * 4 + 4 * bm * bf * 4
    return pl.pallas_call(
        _ffn_kernel,
        grid=(m // bm, nf),
        in_specs=[pl.BlockSpec((d, bf), lambda i, j: (0, j)),
                  pl.BlockSpec((d, bf), lambda i, j: (0, j + nf)),
                  pl.BlockSpec((bf, d), lambda i, j: (j, 0)),
                  row(), full(), full()],
        out_specs=row(),
        out_shape=jax.ShapeDtypeStruct((m, d), F32),
        scratch_shapes=[pltpu.VMEM((bm, d), F32), pltpu.VMEM((bm, bf), BF16),
                        pltpu.VMEM((bm, d), BF16)],
        compiler_params=_params(("parallel", "arbitrary"), vmem + (4 << 20)),
        name="swiglu_ffn_ln",
    )(w_gu, w_gu, w_d, h, g, b)


_QKV_BLOCK = (1024, 1536)
_KV_BLOCK = (1024, 1024)
_LN_BLOCK_ROWS = 512
_FFN_BLOCK_COLS = 512


def kernel(x, mem, w_in, sinks, g_swa, g_sb, w_o, ln1_g, ln1_b, w_q_mem, w_kv_mem,
           w_o_mem, ln2_g, ln2_b, w_gate_up, w_down, ln3_g, ln3_b):
    bsz, s, d = x.shape
    t = bsz * s
    mlen = mem.shape[1]
    h = x.reshape(t, d)
    for l in range(DEPTH):
        row = lambda a: a[l].reshape(1, -1)
        qkv = _matmul(h, w_in[l].astype(BF16), *_QKV_BLOCK, BF16, "qkv_proj")
        qkv = qkv.reshape(bsz, s, D_IN)
        o_a = _swa_attention(qkv, sinks[l], row(g_swa))
        o_b, (wo_b, wq_b, wkv_b, wom_b, wgu_b, wd_b) = _sb_attention(
            qkv, row(g_sb),
            [w_o[l], w_q_mem[l], w_kv_mem[l], w_o_mem[l], w_gate_up[l], w_down[l]])
        h, q = _proj_ln_q(o_a.reshape(t, D_SWA), o_b.reshape(t, D_SB), wo_b,
                          h, row(ln1_g), row(ln1_b), wq_b, _LN_BLOCK_ROWS)
        kv = _matmul(mem.reshape(bsz * mlen, d), wkv_b, *_KV_BLOCK, BF16, "mem_kv_proj")
        h3 = _xattn_ln(q.reshape(bsz, s, d), kv.reshape(bsz, mlen, 2 * d),
                            wom_b, h.reshape(bsz, s, d), row(ln2_g), row(ln2_b), _LN_BLOCK_ROWS)
        h = h3.reshape(t, d)
        h = _ffn_ln(wgu_b, wd_b, h, row(ln3_g), row(ln3_b), _LN_BLOCK_ROWS, _FFN_BLOCK_COLS)
    return h.reshape(bsz, s, d)
```
